```python
import math
import jax, jax.numpy as jnp
from jax import lax
import numpy as np

D_MODEL = 1024
BATCH = 32
SEQ = 256
DEPTH = 2
DEC_BATCH = 8
DEC_SEQ = 2048
PAST_LEN = 512

GRID_W = 64
HA = 4
DHA = 64
A_W = HA * 2 * DHA
POOL_WINDOWS = (2, 4, 8, 16)
POOL_G = 4
POOL_GC = 128
POOL_W = POOL_G * POOL_GC
HC = 8
KVC = 2
REPC = HC // KVC
DHC = 64
WINDOW = 128
C_W = HC * DHC
N_BRANCH = 3
BRANCH_W = 512
QBLK = 128
SPLITS = (A_W, A_W, A_W, POOL_W, C_W, KVC * DHC, KVC * DHC)
IN_W = sum(SPLITS)
ROPE_BASE = 10000.0
ROT_HALF = 32
D_FF = 2816
N_EXP = 8
TOP_K = 2
D_FF_E = 1408
N_DENSE = (DEPTH + 1) // 2
N_MOE = DEPTH // 2
EPS = 1e-6
NEG_INF = -1e30

kernel_name = "hybrid_diffusion_prefix_step"

F32 = jnp.float32


def rmsnorm(x, g):
    xf = x.astype(F32)
    y = xf * lax.rsqrt(jnp.mean(xf * xf, axis=-1, keepdims=True) + EPS)
    return (y * g.astype(F32)).astype(x.dtype)


def adaln_mods(cvec, w, b):
    m = jax.nn.silu(cvec) @ w + b
    return [t[:, None, :] for t in jnp.split(m, 6, axis=-1)]


def split_proj(p):
    B, T, _ = p.shape
    offs = [int(o) for o in np.cumsum(SPLITS)[:-1]]
    qa, ka, va, ub, qc, kc, vc = jnp.split(p, offs, axis=-1)
    return (qa.reshape(B, T, HA, 2, DHA), ka.reshape(B, T, HA, 2, DHA),
            va.reshape(B, T, HA, 2 * DHA), ub,
            qc.reshape(B, T, HC, DHC), kc.reshape(B, T, KVC, DHC), vc.reshape(B, T, KVC, DHC))


def axial_angles(T):
    rows = T // GRID_W
    row = jnp.repeat(jnp.arange(rows, dtype=F32), GRID_W)
    col = jnp.tile(jnp.arange(GRID_W, dtype=F32), rows)
    inv = ROPE_BASE ** (-jnp.arange(0, ROT_HALF, 2, dtype=F32) / ROT_HALF)
    return row[:, None] * inv, col[:, None] * inv


def axial_rope(x, ang_r, ang_c):
    half = ROT_HALF // 2
    bshape = (1, x.shape[1]) + (1,) * (x.ndim - 3) + (half,)

    def rot(seg, ang):
        cos = jnp.cos(ang).reshape(bshape).astype(x.dtype)
        sin = jnp.sin(ang).reshape(bshape).astype(x.dtype)
        a1, a2 = seg[..., :half], seg[..., half:]
        return jnp.concatenate([a1 * cos - a2 * sin, a2 * cos + a1 * sin], axis=-1)

    return jnp.concatenate([rot(x[..., :ROT_HALF], ang_r), rot(x[..., ROT_HALF:], ang_c)], axis=-1)


def diff_lambda(lam_p, l):
    lam_init = 0.8 - 0.6 * math.exp(-0.3 * l)
    lf = lam_p.astype(F32)
    lam = jnp.exp(jnp.sum(lf[0] * lf[1])) - jnp.exp(jnp.sum(lf[2] * lf[3])) + lam_init
    return lam, lam_init


def diff_attention(q, k, v, lam):
    B, Tq = q.shape[:2]
    nblk = Tq // QBLK
    qb = q.reshape(B, nblk, QBLK, HA, 2, DHA).swapaxes(0, 1)

    def block(qi):
        s = jnp.einsum('bqhmd,bkhmd->bhmqk', qi, k).astype(F32) * (DHA ** -0.5)
        p = jax.nn.softmax(s, axis=-1)
        a = p[:, :, 0] - lam * p[:, :, 1]
        return jnp.einsum('bhqk,bkhe->bqhe', a.astype(v.dtype), v)

    o = lax.map(block, qb)
    return o.swapaxes(0, 1).reshape(B, Tq, HA, 2 * DHA)


def sink_softmax_values(s, sink, v):
    sk = jnp.broadcast_to(sink.astype(F32).reshape(1, KVC, REPC, 1, 1), s.shape[:-1] + (1,))
    p = jax.nn.softmax(jnp.concatenate([s, sk], axis=-1), axis=-1)[..., :-1]
    return jnp.einsum('bgrqk,bkgd->bqgrd', p.astype(v.dtype), v)


def gqa_dense_attention(q, k, v, sink):
    B, T = q.shape[:2]
    nblk = T // QBLK
    qb = q.reshape(B, nblk, QBLK, KVC, REPC, DHC).swapaxes(0, 1)

    def block(qi):
        s = jnp.einsum('bqgrd,bkgd->bgrqk', qi, k).astype(F32) * (DHC ** -0.5)
        return sink_softmax_values(s, sink, v)

    o = lax.map(block, qb)
    return o.swapaxes(0, 1).reshape(B, T, C_W)


def window_attention(q, k, v, ck, cv, sink):
    B, T = q.shape[:2]
    L = ck.shape[1]
    nblk = T // QBLK
    span = QBLK + 2 * WINDOW
    kp = jnp.pad(k, ((0, 0), (WINDOW, WINDOW), (0, 0), (0, 0)))
    vp = jnp.pad(v, ((0, 0), (WINDOW, WINDOW), (0, 0), (0, 0)))
    qb = q.reshape(B, nblk, QBLK, KVC, REPC, DHC).swapaxes(0, 1)
    starts = jnp.arange(nblk, dtype=jnp.int32) * QBLK

    def block(args):
        qi, s0 = args
        kw = lax.dynamic_slice_in_dim(kp, s0, span, axis=1)
        vw = lax.dynamic_slice_in_dim(vp, s0, span, axis=1)
        keys = jnp.concatenate([ck, kw], axis=1)
        vals = jnp.concatenate([cv, vw], axis=1)
        s = jnp.einsum('bqgrd,bkgd->bgrqk', qi, keys).astype(F32) * (DHC ** -0.5)
        qpos = s0 + jnp.arange(QBLK)
        kpos = s0 - WINDOW + jnp.arange(span)
        band = ((jnp.abs(qpos[:, None] - kpos[None, :]) <= WINDOW)
                & (kpos >= 0)[None, :] & (kpos < T)[None, :])
        mask = jnp.concatenate([jnp.ones((QBLK, L), dtype=bool), band], axis=1)
        s = jnp.where(mask, s, NEG_INF)
        return sink_softmax_values(s, sink, vals)

    o = lax.map(block, (qb, starts))
    return o.swapaxes(0, 1).reshape(B, T, C_W)


def multiscale_pool(u, w_pool, pscale):
    B, T, _ = u.shape
    uf = u.astype(F32)
    cs = jnp.concatenate([jnp.zeros((B, 1, POOL_W), F32), jnp.cumsum(uf, axis=1)], axis=1)
    t = jnp.arange(T)
    outs = []
    for g, w in enumerate(POOL_WINDOWS):
        lo = jnp.clip(t - w // 2, 0, T)
        hi = jnp.clip(t + w - w // 2, 0, T)
        seg = cs[..., g * POOL_GC:(g + 1) * POOL_GC]
        mean = (jnp.take(seg, hi, axis=1) - jnp.take(seg, lo, axis=1)) / (hi - lo).astype(F32)[None, :, None]
        outs.append(mean - uf[..., g * POOL_GC:(g + 1) * POOL_GC])
    pooled = jnp.stack(outs, axis=2).astype(u.dtype)
    y = jnp.einsum('btgc,gce->btge', pooled, w_pool).reshape(B, T, POOL_W)
    return y * pscale


def merge_branches(h, oA, oB, oC, w_gate_l, w_branch_l, w_out_l):
    B, T, _ = h.shape
    o = jnp.stack([oA, oB, oC], axis=2)
    p = jnp.einsum('btnc,ncd->btnd', o, w_branch_l)
    g = jax.nn.sigmoid((h @ w_gate_l).reshape(B, T, N_BRANCH, D_MODEL))
    return jnp.sum(g * p, axis=2) @ w_out_l


def swiglu(h, wg, wu, wd):
    return (jax.nn.silu(h @ wg) * (h @ wu)) @ wd


def channel_mixer(h, l, w_ff_gate, w_ff_up, w_ff_down, w_router, w_ex_gate, w_ex_up, w_ex_down):
    i = l // 2
    if l % 2 == 0:
        return swiglu(h, w_ff_gate[i], w_ff_up[i], w_ff_down[i])
    logits = (h @ w_router[i]).astype(F32)
    top_v, top_i = lax.top_k(logits, TOP_K)
    top_w = jax.nn.softmax(top_v, axis=-1)
    gates = jnp.sum(jax.nn.one_hot(top_i, N_EXP, dtype=F32) * top_w[..., None], axis=-2).astype(h.dtype)
    y = jnp.zeros_like(h)
    for e in range(N_EXP):
        y = y + gates[..., e:e + 1] * swiglu(h, w_ex_gate[i, e], w_ex_up[i, e], w_ex_down[i, e])
    return y


def setup_inputs(seed: int = 0) -> dict:
    key = jax.random.key(seed)
    ks = iter(jax.random.split(key, 40))

    def nrm(shape, scale=1.0):
        return jax.random.normal(next(ks), shape, F32) * scale

    def gain(shape):
        return 1.0 + nrm(shape, 0.05)

    D = D_MODEL
    return {
        "x_prompt": nrm((BATCH, SEQ, D)),
        "x_sample": nrm((DEC_BATCH, DEC_SEQ, D)),
        "cache_diff_k": nrm((DEC_BATCH, DEPTH, PAST_LEN, HA, 2, DHA)),
        "cache_diff_v": nrm((DEC_BATCH, DEPTH, PAST_LEN, HA, 2 * DHA)),
        "cache_win_k": nrm((DEC_BATCH, DEPTH, PAST_LEN, KVC, DHC)),
        "cache_win_v": nrm((DEC_BATCH, DEPTH, PAST_LEN, KVC, DHC)),
        "c": nrm((DEC_BATCH, D)),
        "c_ctx": nrm((D,)),
        "w_ada": nrm((DEPTH, D, 6 * D), 0.5 * D ** -0.5),
        "b_ada": nrm((DEPTH, 6 * D), 0.01),
        "norm1_g": gain((DEPTH, D)),
        "norm2_g": gain((DEPTH, D)),
        "w_in": nrm((DEPTH, D, IN_W), D ** -0.5),
        "w_gate": nrm((DEPTH, D, N_BRANCH * D), D ** -0.5),
        "lam_p": nrm((DEPTH, 4, DHA), 0.1),
        "subln_g": gain((DEPTH, 2 * DHA)),
        "w_pool": nrm((DEPTH, POOL_G, POOL_GC, POOL_GC), POOL_GC ** -0.5),
        "pool_scale": gain((DEPTH, POOL_W)),
        "sink": nrm((DEPTH, HC), 0.5),
        "w_branch": nrm((DEPTH, N_BRANCH, BRANCH_W, D), BRANCH_W ** -0.5),
        "w_out": nrm((DEPTH, D, D), D ** -0.5),
        "w_ff_gate": nrm((N_DENSE, D, D_FF), D ** -0.5),
        "w_ff_up": nrm((N_DENSE, D, D_FF), D ** -0.5),
        "w_ff_down": nrm((N_DENSE, D_FF, D), D_FF ** -0.5),
        "w_router": nrm((N_MOE, D, N_EXP), D ** -0.5),
        "w_ex_gate": nrm((N_MOE, N_EXP, D, D_FF_E), D ** -0.5),
        "w_ex_up": nrm((N_MOE, N_EXP, D, D_FF_E), D ** -0.5),
        "w_ex_down": nrm((N_MOE, N_EXP, D_FF_E, D), D_FF_E ** -0.5),
        "final_g": gain((D,)),
    }


def reference(x_prompt, x_sample, cache_diff_k, cache_diff_v, cache_win_k, cache_win_v, c, c_ctx,
              w_ada, b_ada, norm1_g, norm2_g, w_in, w_gate, lam_p, subln_g, w_pool, pool_scale,
              sink, w_branch, w_out, w_ff_gate, w_ff_up, w_ff_down, w_router, w_ex_gate,
              w_ex_up, w_ex_down, final_g):
    xp = x_prompt
    Bp, S, _ = xp.shape
    dk_l, dv_l, wk_l, wv_l = [], [], [], []
    for l in range(DEPTH):
        sh1, sc1, g1, sh2, sc2, g2 = adaln_mods(c_ctx[None, :], w_ada[l], b_ada[l])
        h = rmsnorm(xp, norm1_g[l]) * (1.0 + sc1) + sh1
        qa, ka, va, ub, qc, kc, vc = split_proj(h @ w_in[l])
        lam, lam_init = diff_lambda(lam_p[l], l)
        oA = rmsnorm(diff_attention(qa, ka, va, lam), subln_g[l]) * (1.0 - lam_init)
        oA = oA.reshape(Bp, S, A_W)
        oB = multiscale_pool(ub, w_pool[l], pool_scale[l])
        oC = gqa_dense_attention(qc, kc, vc, sink[l])
        xp = xp + g1 * merge_branches(h, oA, oB, oC, w_gate[l], w_branch[l], w_out[l])
        h2 = rmsnorm(xp, norm2_g[l]) * (1.0 + sc2) + sh2
        xp = xp + g2 * channel_mixer(h2, l, w_ff_gate, w_ff_up, w_ff_down, w_router,
                                     w_ex_gate, w_ex_up, w_ex_down)
        dk_l.append(ka)
        dv_l.append(va)
        wk_l.append(kc)
        wv_l.append(vc)
    y_prompt = rmsnorm(xp, final_g)
    new_diff_k = jnp.stack(dk_l, axis=1)
    new_diff_v = jnp.stack(dv_l, axis=1)
    new_win_k = jnp.stack(wk_l, axis=1)
    new_win_v = jnp.stack(wv_l, axis=1)

    xs = x_sample
    Bs, T, _ = xs.shape
    ang_r, ang_c = axial_angles(T)
    for l in range(DEPTH):
        sh1, sc1, g1, sh2, sc2, g2 = adaln_mods(c, w_ada[l], b_ada[l])
        h = rmsnorm(xs, norm1_g[l]) * (1.0 + sc1) + sh1
        qa, ka, va, ub, qc, kc, vc = split_proj(h @ w_in[l])
        qa = axial_rope(qa, ang_r, ang_c)
        ka = axial_rope(ka, ang_r, ang_c)
        qc = axial_rope(qc, ang_r, ang_c)
        kc = axial_rope(kc, ang_r, ang_c)
        lam, lam_init = diff_lambda(lam_p[l], l)
        ka_all = jnp.concatenate([cache_diff_k[:, l], ka], axis=1)
        va_all = jnp.concatenate([cache_diff_v[:, l], va], axis=1)
        oA = rmsnorm(diff_attention(qa, ka_all, va_all, lam), subln_g[l]) * (1.0 - lam_init)
        oA = oA.reshape(Bs, T, A_W)
        oB = multiscale_pool(ub, w_pool[l], pool_scale[l])
        oC = window_attention(qc, kc, vc, cache_win_k[:, l], cache_win_v[:, l], sink[l])
        xs = xs + g1 * merge_branches(h, oA, oB, oC, w_gate[l], w_branch[l], w_out[l])
        h2 = rmsnorm(xs, norm2_g[l]) * (1.0 + sc2) + sh2
        xs = xs + g2 * channel_mixer(h2, l, w_ff_gate, w_ff_up, w_ff_down, w_router,
                                     w_ex_gate, w_ex_up, w_ex_down)
    y_sample = rmsnorm(xs, final_g)

    return (y_prompt, y_sample, new_diff_k, new_diff_v, new_win_k, new_win_v)
```

```python
import functools
import math

import numpy as np
import jax
import jax.numpy as jnp
from jax import lax
from jax.experimental import pallas as pl
from jax.experimental.pallas import tpu as pltpu

F32 = jnp.float32
BF16 = jnp.bfloat16

D_MODEL = 1024
DEPTH = 2
GRID_W = 64
HA = 4
DHA = 64
HC = 8
KVC = 2
REPC = HC // KVC
DHC = 64
WINDOW = 128
QBLK = 128
POOL_WINDOWS = (2, 4, 8, 16)
POOL_GC = 128
POOL_W = 512
BRANCH_W = 512
N_BRANCH = 3
ROPE_BASE = 10000.0
ROT_HALF = 32
N_EXP = 8
D_FF_E = 1408
EPS = 1e-6
NEG_INF = -1e30

LANE = 128
MOD_ROWS = 16
P_W = 20 * LANE
IN_W2 = 24 * LANE
VMEM_LIMIT = 56 * 1024 * 1024


def _cparams(sem):
    return pltpu.CompilerParams(dimension_semantics=sem, vmem_limit_bytes=VMEM_LIMIT)


def _dot(a, b):
    return jnp.dot(a, b, preferred_element_type=F32)


def _dot_nt(a, b):
    return lax.dot_general(a, b, (((1,), (1,)), ((), ())), preferred_element_type=F32)


def _sigmoid(x):
    return 1.0 / (1.0 + jnp.exp(-x))


def _modnorm(x, g, shift, scale):
    ms = jnp.mean(x * x, axis=-1, keepdims=True)
    return (x * lax.rsqrt(ms + EPS) * g) * (1.0 + scale) + shift


def _adaln_kernel(c_ref, w_ref, b_ref, o_ref):
    c = c_ref[...]
    s = c * _sigmoid(c)
    o_ref[0] = _dot(s.astype(BF16), w_ref[0].astype(BF16)) + b_ref[0]


def _adaln(cvec, w_ada, b_ada):
    tn = 1536
    n = 6 * D_MODEL
    return pl.pallas_call(
        _adaln_kernel,
        out_shape=jax.ShapeDtypeStruct((DEPTH, MOD_ROWS, n), F32),
        grid=(DEPTH, n // tn),
        in_specs=[
            pl.BlockSpec((MOD_ROWS, D_MODEL), lambda l, j: (0, 0)),
            pl.BlockSpec((1, D_MODEL, tn), lambda l, j: (l, 0, j)),
            pl.BlockSpec((1, 1, tn), lambda l, j: (l, 0, j)),
        ],
        out_specs=pl.BlockSpec((1, MOD_ROWS, tn), lambda l, j: (l, 0, j)),
        compiler_params=_cparams(("arbitrary", "arbitrary")),
        name="adaln",
    )(cvec, w_ada, b_ada.reshape(DEPTH, 1, n))


def _rope(p, cos, sin_next, sin_prev):
    nxt = pltpu.roll(p, LANE - 16, 1)
    prv = pltpu.roll(p, 16, 1)
    return p * cos + nxt * sin_next + prv * sin_prev


def _inproj_kernel(*refs, rope):
    if rope:
        (x_ref, g_ref, mod_ref, w_ref, cos_ref, sn_ref, sp_ref, p_ref, ub_ref) = refs
    else:
        (x_ref, g_ref, mod_ref, w_ref, p_ref, ub_ref, dk_ref, dv_ref, wk_ref, wv_ref) = refs
    h = _modnorm(x_ref[...], g_ref[...], mod_ref[0, 0:1, :], mod_ref[0, 1:2, :])
    hb = h.astype(BF16)
    if rope:
        cos, sn, sp = cos_ref[...], sn_ref[...], sp_ref[...]
    lane = lax.broadcasted_iota(jnp.int32, (1, LANE), 1)
    for c in range(IN_W2 // 512):
        pc = _dot(hb, w_ref[:, c * 512:(c + 1) * 512])
        blocks = [pc[:, j * LANE:(j + 1) * LANE] for j in range(4)]
        if c == 5:
            ub_ref[...] = pc
            continue
        if rope:
            n_rot = {0: 4, 1: 4, 2: 0, 3: 4, 4: 2}[c]
            blocks = [_rope(b, cos, sn, sp) if j < n_rot else b for j, b in enumerate(blocks)]
        elif c == 1:
            dk_ref[...] = pc
        elif c == 2:
            dv_ref[...] = pc
        elif c == 4:
            wk_ref[...] = jnp.where(lane < DHC, blocks[0], blocks[1])
            wv_ref[...] = jnp.where(lane < DHC, blocks[2], blocks[3])
        for j in range(4):
            p_ref[:, (4 * c + j) * LANE:(4 * c + j + 1) * LANE] = blocks[j].astype(BF16)


def _inproj(x, g, mods, w, tm, rows_per_mod, rope_tabs=None):
    m = x.shape[0]
    rope = rope_tabs is not None
    mod_map = (lambda i: (i // (rows_per_mod // tm), 0, 0)) if rows_per_mod else (lambda i: (0, 0, 0))
    in_specs = [
        pl.BlockSpec((tm, D_MODEL), lambda i: (i, 0)),
        pl.BlockSpec((1, D_MODEL), lambda i: (0, 0)),
        pl.BlockSpec((1, 6, D_MODEL), mod_map),
        pl.BlockSpec((D_MODEL, IN_W2), lambda i: (0, 0)),
    ]
    args = [x, g, mods, w]
    out_shape = [jax.ShapeDtypeStruct((m, P_W), BF16), jax.ShapeDtypeStruct((m, POOL_W), F32)]
    out_specs = [pl.BlockSpec((tm, P_W), lambda i: (i, 0)), pl.BlockSpec((tm, POOL_W), lambda i: (i, 0))]
    if rope:
        t = rope_tabs[0].shape[0]
        for tab in rope_tabs:
            in_specs.append(pl.BlockSpec((tm, LANE), lambda i: (i % (t // tm), 0)))
            args.append(tab)
    else:
        for wdt in (512, 512, LANE, LANE):
            out_shape.append(jax.ShapeDtypeStruct((m, wdt), F32))
            out_specs.append(pl.BlockSpec((tm, wdt), lambda i: (i, 0)))
    return pl.pallas_call(
        functools.partial(_inproj_kernel, rope=rope),
        out_shape=out_shape,
        grid=(m // tm,),
        in_specs=in_specs,
        out_specs=out_specs,
        compiler_params=_cparams(("arbitrary",)),
        name="inproj_rope" if rope else "inproj",
    )(*args)


def _diff_lambda(lam_ref, lam_init):
    lf = lam_ref[...]
    s1 = jnp.sum(lf[0:1] * lf[1:2], axis=-1, keepdims=True)
    s2 = jnp.sum(lf[2:3] * lf[3:4], axis=-1, keepdims=True)
    return jnp.exp(s1) - jnp.exp(s2) + lam_init


def _diff_core(q, k, v, lam, g, lam_init):
    tq = q.shape[0]
    lane = lax.broadcasted_iota(jnp.int32, (1, LANE), 1)
    zero = jnp.zeros_like(q)
    q2 = jnp.concatenate([jnp.where(lane < DHA, q, zero), jnp.where(lane < DHA, zero, q)], axis=0)
    s = _dot_nt(q2, k)
    m = jnp.max(s, axis=-1, keepdims=True)
    e = jnp.exp(s - m)
    p = e * (1.0 / jnp.sum(e, axis=-1, keepdims=True))
    a = p[:tq] - lam * p[tq:]
    o = _dot(a.astype(BF16), v)
    ms = jnp.mean(o * o, axis=-1, keepdims=True)
    return ((o * lax.rsqrt(ms + EPS) * g) * (1.0 - lam_init)).astype(BF16)


def _diff_ctx_kernel(q_ref, k_ref, v_ref, lam_ref, g_ref, o_ref, *, lam_init):
    lam = _diff_lambda(lam_ref, lam_init)
    o_ref[...] = _diff_core(q_ref[...], k_ref[...], v_ref[...], lam, g_ref[...], lam_init)


def _diff_ctx(p, lam_p, subln_g, lam_init, nb, s):
    return pl.pallas_call(
        functools.partial(_diff_ctx_kernel, lam_init=lam_init),
        out_shape=jax.ShapeDtypeStruct((nb * s, HA * LANE), BF16),
        grid=(nb, HA),
        in_specs=[
            pl.BlockSpec((s, LANE), lambda b, h: (b, h)),
            pl.BlockSpec((s, LANE), lambda b, h: (b, 4 + h)),
            pl.BlockSpec((s, LANE), lambda b, h: (b, 8 + h)),
            pl.BlockSpec((4, DHA), lambda b, h: (0, 0)),
            pl.BlockSpec((1, LANE), lambda b, h: (0, 0)),
        ],
        out_specs=pl.BlockSpec((s, LANE), lambda b, h: (b, h)),
        compiler_params=_cparams(("arbitrary", "arbitrary")),
        name="diff_ctx",
    )(p, p, p, lam_p, subln_g)


def _diff_lat_kernel(q_ref, kn_ref, vn_ref, kc_ref, vc_ref, lam_ref, g_ref, o_ref,
                     k_all, v_all, *, lam_init, past):
    @pl.when(pl.program_id(2) == 0)
    def _():
        k_all[0:past, :] = kc_ref[0, 0].astype(BF16)
        v_all[0:past, :] = vc_ref[0, 0].astype(BF16)
        k_all[past:, :] = kn_ref[...]
        v_all[past:, :] = vn_ref[...]

    lam = _diff_lambda(lam_ref, lam_init)
    o_ref[...] = _diff_core(q_ref[...], k_all[...], v_all[...], lam, g_ref[...], lam_init)


def _diff_lat(p, cache_k, cache_v, l, lam_p, subln_g, lam_init, nb, t, tq):
    past = cache_k.shape[2]
    nq = t // tq
    return pl.pallas_call(
        functools.partial(_diff_lat_kernel, lam_init=lam_init, past=past),
        out_shape=jax.ShapeDtypeStruct((nb * t, HA * LANE), BF16),
        grid=(nb, HA, nq),
        in_specs=[
            pl.BlockSpec((tq, LANE), lambda b, h, i: (b * nq + i, h)),
            pl.BlockSpec((t, LANE), lambda b, h, i: (b, 4 + h)),
            pl.BlockSpec((t, LANE), lambda b, h, i: (b, 8 + h)),
            pl.BlockSpec((1, 1, past, LANE), lambda b, h, i: (b, l, 0, h)),
            pl.BlockSpec((1, 1, past, LANE), lambda b, h, i: (b, l, 0, h)),
            pl.BlockSpec((4, DHA), lambda b, h, i: (0, 0)),
            pl.BlockSpec((1, LANE), lambda b, h, i: (0, 0)),
        ],
        out_specs=pl.BlockSpec((tq, LANE), lambda b, h, i: (b * nq + i, h)),
        scratch_shapes=[pltpu.VMEM((past + t, LANE), BF16), pltpu.VMEM((past + t, LANE), BF16)],
        compiler_params=_cparams(("arbitrary", "arbitrary", "arbitrary")),
        name="diff_lat",
    )(p, p, p, cache_k, cache_v, lam_p, subln_g)


POOL_PAD = 16


def _pool_kernel(u_ref, o_ref, pad_ref, *, t):
    tp = t + POOL_PAD
    pos = lax.broadcasted_iota(jnp.int32, (t, 1), 0)
    for gi, w in enumerate(POOL_WINDOWS):
        sl = slice(gi * POOL_GC, (gi + 1) * POOL_GC)
        u = u_ref[0, :, sl]
        pad_ref[0:t, :] = u
        pad_ref[t:tp, :] = jnp.zeros((POOL_PAD, POOL_GC), F32)
        acc = pad_ref[...]
        step = 1
        while step < w:
            acc = acc + pltpu.roll(acc, step, 0)
            step *= 2
        ahead = w - w // 2 - 1
        if ahead:
            acc = pltpu.roll(acc, tp - ahead, 0)
        lo = jnp.maximum(pos - w // 2, 0)
        hi = jnp.minimum(pos + (w - w // 2), t)
        mean = acc[0:t] / (hi - lo).astype(F32)
        o_ref[0, :, sl] = (mean - u).astype(BF16)


def _pool(ub, nb, t):
    return pl.pallas_call(
        functools.partial(_pool_kernel, t=t),
        out_shape=jax.ShapeDtypeStruct((nb, t, POOL_W), BF16),
        grid=(nb,),
        in_specs=[pl.BlockSpec((1, t, POOL_W), lambda b: (b, 0, 0))],
        out_specs=pl.BlockSpec((1, t, POOL_W), lambda b: (b, 0, 0)),
        scratch_shapes=[pltpu.VMEM((t + POOL_PAD, POOL_GC), F32)],
        compiler_params=_cparams(("arbitrary",)),
        name="pool",
    )(ub.reshape(nb, t, POOL_W)).reshape(nb * t, POOL_W)


def _stack_heads(q):
    lane = lax.broadcasted_iota(jnp.int32, (1, LANE), 1)
    zero = jnp.zeros_like(q[:, :LANE])
    parts = []
    for pair in range(2):
        qp = q[:, pair * LANE:(pair + 1) * LANE]
        parts.append(jnp.where(lane < DHC, qp, zero))
        parts.append(jnp.where(lane < DHC, zero, qp))
    return jnp.concatenate(parts, axis=0)


def _gqa_core(score_parts, masks, value_parts, sink_ref, g, tq):
    lane = lax.broadcasted_iota(jnp.int32, (1, LANE), 1)
    outs = []
    for r in range(REPC):
        rows = slice(r * tq, (r + 1) * tq)
        sink = sink_ref[g * REPC + r]
        parts = [s[rows] if mk is None else jnp.where(mk, s[rows], NEG_INF)
                 for s, mk in zip(score_parts, masks)]
        m = jnp.maximum(functools.reduce(jnp.maximum, [jnp.max(s, axis=-1, keepdims=True) for s in parts]), sink)
        es = [jnp.exp(s - m) for s in parts]
        den = functools.reduce(lambda a, b: a + b, [jnp.sum(e, axis=-1, keepdims=True) for e in es])
        inv = 1.0 / (den + jnp.exp(sink - m))
        o = functools.reduce(lambda a, b: a + b,
                             [_dot((e * inv).astype(BF16), v) for e, v in zip(es, value_parts)])
        outs.append(o)
    return jnp.concatenate([jnp.where(lane < DHC, outs[0], outs[1]),
                            jnp.where(lane < DHC, outs[2], outs[3])], axis=1).astype(BF16)


def _gqa_ctx_kernel(sink_ref, q_ref, k_ref, v_ref, o_ref, *, s):
    g = pl.program_id(1)
    q4 = _stack_heads(q_ref[...])
    sc = _dot_nt(q4, k_ref[...])
    o_ref[...] = _gqa_core([sc], [None], [v_ref[...]], sink_ref, g, s)


def _gqa_ctx(p, sink, nb, s):
    return pl.pallas_call(
        functools.partial(_gqa_ctx_kernel, s=s),
        out_shape=jax.ShapeDtypeStruct((nb * s, HC * DHC), BF16),
        grid=(nb, KVC),
        in_specs=[
            pl.BlockSpec(memory_space=pltpu.SMEM),
            pl.BlockSpec((s, 2 * LANE), lambda b, g: (b, 6 + g)),
            pl.BlockSpec((s, LANE), lambda b, g: (b, 16 + g)),
            pl.BlockSpec((s, LANE), lambda b, g: (b, 18 + g)),
        ],
        out_specs=pl.BlockSpec((s, 2 * LANE), lambda b, g: (b, g)),
        compiler_params=_cparams(("arbitrary", "arbitrary")),
        name="gqa_ctx",
    )(sink, p, p, p)


def _gqa_lat_kernel(sink_ref, q_ref, kn_ref, vn_ref, kc_ref, vc_ref, o_ref, kc_dup, vc_dup, *, t):
    g = pl.program_id(1)
    i = pl.program_id(2)
    lane = lax.broadcasted_iota(jnp.int32, (1, LANE), 1)

    @pl.when(i == 0)
    def _():
        keep = jnp.where(lane < DHC, 0, 1) == g
        for src, dst in ((kc_ref, kc_dup), (vc_ref, vc_dup)):
            blk = src[0, 0]
            dst[...] = jnp.where(keep, blk, pltpu.roll(blk, DHC, 1)).astype(BF16)

    span = QBLK + 2 * WINDOW
    w0 = pl.multiple_of(jnp.clip((i - 1) * QBLK, 0, t - span), QBLK)
    kw = kn_ref[pl.ds(w0, span), :]
    vw = vn_ref[pl.ds(w0, span), :]
    q4 = _stack_heads(q_ref[...])
    s_ctx = _dot_nt(q4, kc_dup[...])
    s_win = _dot_nt(q4, kw)
    qpos = i * QBLK + lax.broadcasted_iota(jnp.int32, (QBLK, span), 0)
    kpos = w0 + lax.broadcasted_iota(jnp.int32, (QBLK, span), 1)
    band = jnp.abs(qpos - kpos) <= WINDOW
    o_ref[...] = _gqa_core([s_ctx, s_win], [None, band], [vc_dup[...], vw], sink_ref, g, QBLK)


def _gqa_lat(p, cache_k, cache_v, l, sink, nb, t):
    past = cache_k.shape[2]
    nq = t // QBLK
    return pl.pallas_call(
        functools.partial(_gqa_lat_kernel, t=t),
        out_shape=jax.ShapeDtypeStruct((nb * t, HC * DHC), BF16),
        grid=(nb, KVC, nq),
        in_specs=[
            pl.BlockSpec(memory_space=pltpu.SMEM),
            pl.BlockSpec((QBLK, 2 * LANE), lambda b, g, i: (b * nq + i, 6 + g)),
            pl.BlockSpec((t, LANE), lambda b, g, i: (b, 16 + g)),
            pl.BlockSpec((t, LANE), lambda b, g, i: (b, 18 + g)),
            pl.BlockSpec((1, 1, past, LANE), lambda b, g, i: (b, l, 0, 0)),
            pl.BlockSpec((1, 1, past, LANE), lambda b, g, i: (b, l, 0, 0)),
        ],
        out_specs=pl.BlockSpec((QBLK, 2 * LANE), lambda b, g, i: (b * nq + i, g)),
        scratch_shapes=[pltpu.VMEM((past, LANE), BF16), pltpu.VMEM((past, LANE), BF16)],
        compiler_params=_cparams(("arbitrary", "arbitrary", "arbitrary")),
        name="gqa_lat",
    )(sink, p, p, p, cache_k, cache_v)


def _merge_kernel(*refs, router):
    (x_ref, oa_ref, pb_ref, oc_ref, g1_ref, g2_ref, mod_ref, wgate_ref, wpool_ref, ps_ref,
     wbr_ref, wout_ref) = refs[:12]
    if router:
        wr_hi_ref, wr_lo_ref, xo_ref, h2_ref, lg_ref = refs[12:]
    else:
        xo_ref, h2_ref = refs[12:]
    x = x_ref[...]
    h = _modnorm(x, g1_ref[...], mod_ref[0, 0:1, :], mod_ref[0, 1:2, :])
    hb = h.astype(BF16)
    pooled = pb_ref[...]
    yb = jnp.concatenate(
        [_dot(pooled[:, j * POOL_GC:(j + 1) * POOL_GC], wpool_ref[j]) for j in range(len(POOL_WINDOWS))],
        axis=1) * ps_ref[...]
    branches = (oa_ref[...], yb.astype(BF16), oc_ref[...])
    acc = None
    for n in range(N_BRANCH):
        gate = _sigmoid(_dot(hb, wgate_ref[:, n * D_MODEL:(n + 1) * D_MODEL]))
        term = gate * _dot(branches[n], wbr_ref[n])
        acc = term if acc is None else acc + term
    xn = x + mod_ref[0, 2:3, :] * _dot(acc.astype(BF16), wout_ref[...])
    xo_ref[...] = xn
    h2 = _modnorm(xn, g2_ref[...], mod_ref[0, 3:4, :], mod_ref[0, 4:5, :])
    h2_hi = h2.astype(BF16)
    h2_ref[...] = h2_hi
    if router:
        h2_lo = (h2 - h2_hi.astype(F32)).astype(BF16)
        w_hi, w_lo = wr_hi_ref[...], wr_lo_ref[...]
        lg_ref[...] = (_dot(h2_hi, w_hi) + (_dot(h2_hi, w_lo) + _dot(h2_lo, w_hi))) + _dot(h2_lo, w_lo)


def _merge(x, oa, pooled, oc, g1, g2, mods, wgate, wpool, pscale, wbr, wout, tm, rows_per_mod, wr=None):
    m = x.shape[0]
    router = wr is not None
    mod_map = (lambda i: (i // (rows_per_mod // tm), 0, 0)) if rows_per_mod else (lambda i: (0, 0, 0))
    row = lambda w: pl.BlockSpec((tm, w), lambda i: (i, 0))
    full = lambda a: pl.BlockSpec(a.shape, lambda i: (0,) * a.ndim)
    in_specs = [row(D_MODEL), row(BRANCH_W), row(BRANCH_W), row(BRANCH_W), full(g1), full(g2),
                pl.BlockSpec((1, 6, D_MODEL), mod_map), full(wgate), full(wpool), full(pscale),
                full(wbr), full(wout)]
    args = [x, oa, pooled, oc, g1, g2, mods, wgate, wpool, pscale, wbr, wout]
    out_shape = [jax.ShapeDtypeStruct((m, D_MODEL), F32), jax.ShapeDtypeStruct((m, D_MODEL), BF16)]
    out_specs = [row(D_MODEL), row(D_MODEL)]
    if router:
        in_specs += [full(wr[0]), full(wr[1])]
        args += [wr[0], wr[1]]
        out_shape.append(jax.ShapeDtypeStruct((m, LANE), F32))
        out_specs.append(row(LANE))
    return pl.pallas_call(
        functools.partial(_merge_kernel, router=router),
        out_shape=out_shape,
        grid=(m // tm,),
        in_specs=in_specs,
        out_specs=out_specs,
        compiler_params=_cparams(("arbitrary",)),
        name="merge",
    )(*args)


def _top2_gates(logits):
    lane = lax.broadcasted_iota(jnp.int32, logits.shape, 1).astype(F32)
    neg = -jnp.inf
    l1 = jnp.where(lane < N_EXP, logits, neg)
    m1 = jnp.max(l1, axis=-1, keepdims=True)
    i1 = jnp.min(jnp.where(l1 == m1, lane, float(LANE)), axis=-1, keepdims=True)
    l2 = jnp.where(lane == i1, neg, l1)
    m2 = jnp.max(l2, axis=-1, keepdims=True)
    i2 = jnp.min(jnp.where(l2 == m2, lane, float(LANE)), axis=-1, keepdims=True)
    e2 = jnp.exp(m2 - m1)
    den = 1.0 + e2
    return jnp.where(lane == i1, 1.0 / den, 0.0) + jnp.where(lane == i2, e2 / den, 0.0)


def _ffn_kernel(*refs, moe, final):
    refs = list(refs)
    h_ref, x_ref, mod_ref = refs[:3]
    k = 3
    if moe:
        lg_ref = refs[k]
        k += 1
    wg_ref, wu_ref, wd_ref = refs[k:k + 3]
    k += 3
    if final:
        fg_ref = refs[k]
        k += 1
    o_ref, acc_ref = refs[k], refs[k + 1]
    if moe:
        gate_ref = refs[k + 2]
    e = pl.program_id(1)

    @pl.when(e == 0)
    def _():
        acc_ref[...] = jnp.zeros_like(acc_ref)
        if moe:
            gate_ref[...] = _top2_gates(lg_ref[...])

    h = h_ref[...]
    a = _dot(h, wg_ref[...])
    u = _dot(h, wu_ref[...])
    y = _dot(((a * _sigmoid(a)) * u).astype(BF16), wd_ref[...])
    if moe:
        lane = lax.broadcasted_iota(jnp.int32, (1, LANE), 1)
        ge = jnp.sum(jnp.where(lane == e, gate_ref[...], 0.0), axis=-1, keepdims=True)
        y = ge * y
    acc_ref[...] += y

    @pl.when(e == pl.num_programs(1) - 1)
    def _():
        xn = x_ref[...] + mod_ref[0, 5:6, :] * acc_ref[...]
        if final:
            ms = jnp.mean(xn * xn, axis=-1, keepdims=True)
            xn = xn * lax.rsqrt(ms + EPS) * fg_ref[...]
        o_ref[...] = xn


def _ffn(h2, x, mods, wg, wu, wd, tm, rows_per_mod, logits=None, final_g=None):
    m = x.shape[0]
    moe = logits is not None
    final = final_g is not None
    mod_map = (lambda i, e: (i // (rows_per_mod // tm), 0, 0)) if rows_per_mod else (lambda i, e: (0, 0, 0))
    row = lambda w: pl.BlockSpec((tm, w), lambda i, e: (i, 0))
    in_specs = [row(D_MODEL), row(D_MODEL), pl.BlockSpec((1, 6, D_MODEL), mod_map)]
    args = [h2, x, mods]
    if moe:
        in_specs.append(row(LANE))
        args.append(logits)
        n_chunks = N_EXP
        in_specs += [pl.BlockSpec((None, D_MODEL, D_FF_E), lambda i, e: (e, 0, 0)),
                     pl.BlockSpec((None, D_MODEL, D_FF_E), lambda i, e: (e, 0, 0)),
                     pl.BlockSpec((None, D_FF_E, D_MODEL), lambda i, e: (e, 0, 0))]
    else:
        n_chunks = wg.shape[1] // D_FF_E
        in_specs += [pl.BlockSpec((D_MODEL, D_FF_E), lambda i, e: (0, e)),
                     pl.BlockSpec((D_MODEL, D_FF_E), lambda i, e: (0, e)),
                     pl.BlockSpec((D_FF_E, D_MODEL), lambda i, e: (e, 0))]
    args += [wg, wu, wd]
    if final:
        in_specs.append(pl.BlockSpec((1, D_MODEL), lambda i, e: (0, 0)))
        args.append(final_g)
    scratch = [pltpu.VMEM((tm, D_MODEL), F32)]
    if moe:
        scratch.append(pltpu.VMEM((tm, LANE), F32))
    return pl.pallas_call(
        functools.partial(_ffn_kernel, moe=moe, final=final),
        out_shape=jax.ShapeDtypeStruct((m, D_MODEL), F32),
        grid=(m // tm, n_chunks),
        in_specs=in_specs,
        out_specs=row(D_MODEL),
        scratch_shapes=scratch,
        compiler_params=_cparams(("arbitrary", "arbitrary")),
        name="moe" if moe else "ffn",
    )(*args)


def _prep_w_in(w):
    qa, ka, va, ub, qc = (w[:, 0:512], w[:, 512:1024], w[:, 1024:1536], w[:, 1536:2048], w[:, 2048:2560])
    kc, vc = w[:, 2560:2688], w[:, 2688:2816]
    dup = lambda a: jnp.concatenate([a[:, 0:64], a[:, 0:64], a[:, 64:128], a[:, 64:128]], axis=1)
    scale = DHA ** -0.5
    return jnp.concatenate([qa * scale, ka, va, qc * scale, dup(kc), dup(vc), ub], axis=1).astype(BF16)


def _rope_tables(t):
    rows = t // GRID_W
    row = jnp.repeat(jnp.arange(rows, dtype=F32), GRID_W)
    col = jnp.tile(jnp.arange(GRID_W, dtype=F32), rows)
    inv = ROPE_BASE ** (-jnp.arange(0, ROT_HALF, 2, dtype=F32) / ROT_HALF)
    ang_r, ang_c = row[:, None] * inv, col[:, None] * inv
    zero = jnp.zeros_like(ang_r)
    seg = lambda a, b: jnp.concatenate([a, b], axis=1)
    cos = jnp.concatenate([seg(jnp.cos(ang_r), jnp.cos(ang_r)), seg(jnp.cos(ang_c), jnp.cos(ang_c))], axis=1)
    s_next = jnp.concatenate([seg(-jnp.sin(ang_r), zero), seg(-jnp.sin(ang_c), zero)], axis=1)
    s_prev = jnp.concatenate([seg(zero, jnp.sin(ang_r)), seg(zero, jnp.sin(ang_c))], axis=1)
    two = lambda a: jnp.concatenate([a, a], axis=1)
    return two(cos), two(s_next), two(s_prev)


def kernel(x_prompt, x_sample, cache_diff_k, cache_diff_v, cache_win_k, cache_win_v, c, c_ctx, w_ada, b_ada, norm1_g, norm2_g, w_in, w_gate, lam_p, subln_g, w_pool, pool_scale, sink, w_branch, w_out, w_ff_gate, w_ff_up, w_ff_down, w_router, w_ex_gate, w_ex_up, w_ex_down, final_g):
    bp, s, _ = x_prompt.shape
    bs, t, _ = x_sample.shape
    past = cache_diff_k.shape[2]

    cvec = jnp.concatenate([c_ctx[None, :], c, jnp.zeros((MOD_ROWS - 1 - bs, D_MODEL), F32)], axis=0)
    mods = _adaln(cvec, w_ada, b_ada).reshape(DEPTH, MOD_ROWS, 6, D_MODEL)
    rope_tabs = _rope_tables(t)

    ck_a = cache_diff_k.reshape(bs, DEPTH, past, HA * 2 * DHA)
    cv_a = cache_diff_v.reshape(bs, DEPTH, past, HA * 2 * DHA)
    ck_c = cache_win_k.reshape(bs, DEPTH, past, KVC * DHC)
    cv_c = cache_win_v.reshape(bs, DEPTH, past, KVC * DHC)

    xc = x_prompt.reshape(bp * s, D_MODEL)
    xl = x_sample.reshape(bs * t, D_MODEL)
    caches = [[], [], [], []]
    for l in range(DEPTH):
        last = l == DEPTH - 1
        lam_init = 0.8 - 0.6 * math.exp(-0.3 * l)
        w_in_l = _prep_w_in(w_in[l])
        wgate_l = w_gate[l].astype(BF16)
        wpool_l = w_pool[l].astype(BF16)
        wbr_l = w_branch[l].astype(BF16)
        wout_l = w_out[l].astype(BF16)
        g1 = norm1_g[l][None, :]
        g2 = norm2_g[l][None, :]
        sg = subln_g[l][None, :]
        ps = pool_scale[l][None, :]
        fg = final_g[None, :] if last else None
        i = l // 2
        if l % 2 == 0:
            wr = None
            wg, wu, wd = (w_ff_gate[i].astype(BF16), w_ff_up[i].astype(BF16), w_ff_down[i].astype(BF16))
        else:
            wr_f = jnp.pad(w_router[i], ((0, 0), (0, LANE - N_EXP)))
            wr_hi = wr_f.astype(BF16)
            wr = (wr_hi, (wr_f - wr_hi.astype(F32)).astype(BF16))
            wg, wu, wd = (w_ex_gate[i].astype(BF16), w_ex_up[i].astype(BF16), w_ex_down[i].astype(BF16))

        m_c = mods[l, 0:1]
        pc, ubc, dk, dv, wk, wv = _inproj(xc, g1, m_c, w_in_l, tm=256, rows_per_mod=0)
        for lst, a in zip(caches, (dk, dv, wk, wv)):
            lst.append(a)
        oa = _diff_ctx(pc, lam_p[l], sg, lam_init, bp, s)
        pooled = _pool(ubc, bp, s)
        oc = _gqa_ctx(pc, sink[l], bp, s)
        res = _merge(xc, oa, pooled, oc, g1, g2, m_c, wgate_l, wpool_l, ps, wbr_l, wout_l,
                     tm=256, rows_per_mod=0, wr=wr)
        xc = _ffn(res[1], res[0], m_c, wg, wu, wd, tm=512, rows_per_mod=0,
                  logits=res[2] if wr else None, final_g=fg)

        m_l = mods[l, 1:1 + bs]
        pl_, ubl = _inproj(xl, g1, m_l, w_in_l, tm=256, rows_per_mod=t, rope_tabs=rope_tabs)
        oa = _diff_lat(pl_, ck_a, cv_a, l, lam_p[l], sg, lam_init, bs, t, tq=128)
        pooled = _pool(ubl, bs, t)
        oc = _gqa_lat(pl_, ck_c, cv_c, l, sink[l], bs, t)
        res = _merge(xl, oa, pooled, oc, g1, g2, m_l, wgate_l, wpool_l, ps, wbr_l, wout_l,
                     tm=256, rows_per_mod=t, wr=wr)
        xl = _ffn(res[1], res[0], m_l, wg, wu, wd, tm=512, rows_per_mod=t,
                  logits=res[2] if wr else None, final_g=fg)

    y_prompt = xc.reshape(bp, s, D_MODEL)
    y_sample = xl.reshape(bs, t, D_MODEL)
    new_diff_k = jnp.stack(caches[0], axis=0).reshape(DEPTH, bp, s, HA, 2, DHA).swapaxes(0, 1)
    new_diff_v = jnp.stack(caches[1], axis=0).reshape(DEPTH, bp, s, HA, 2 * DHA).swapaxes(0, 1)
    new_win_k = jnp.stack(caches[2], axis=0).reshape(DEPTH, bp, s, KVC, DHC).swapaxes(0, 1)
    new_win_v = jnp.stack(caches[3], axis=0).reshape(DEPTH, bp, s, KVC, DHC).swapaxes(0, 1)
    return (y_prompt, y_sample, new_diff_k, new_diff_v, new_win_k, new_win_v)
```

```python
import functools
import math

import numpy as np
import jax
import jax.numpy as jnp
from jax import lax
from jax.experimental import pallas as pl
from jax.experimental.pallas import tpu as pltpu

F32 = jnp.float32
BF16 = jnp.bfloat16

D_MODEL = 1024
DEPTH = 2
GRID_W = 64
HA = 4
DHA = 64
HC = 8
KVC = 2
REPC = HC // KVC
DHC = 64
WINDOW = 128
QBLK = 128
POOL_WINDOWS = (2, 4, 8, 16)
POOL_GC = 128
POOL_W = 512
BRANCH_W = 512
N_BRANCH = 3
ROPE_BASE = 10000.0
ROT_HALF = 32
N_EXP = 8
D_FF_E = 1408
EPS = 1e-6
NEG_INF = -1e30
LOG2E = math.log2(math.e)

LANE = 128
MOD_ROWS = 16
P_W = 20 * LANE
IN_W2 = 24 * LANE
VMEM_LIMIT = 56 * 1024 * 1024


def _cparams(sem):
    return pltpu.CompilerParams(dimension_semantics=sem, vmem_limit_bytes=VMEM_LIMIT)


def _dot(a, b):
    return jnp.dot(a, b, preferred_element_type=F32)


def _dot_nt(a, b):
    return lax.dot_general(a, b, (((1,), (1,)), ((), ())), preferred_element_type=F32)


def _sigmoid(x):
    return 1.0 / (1.0 + jnp.exp(-x))


def _modnorm(x, g, shift, scale):
    ms = jnp.mean(x * x, axis=-1, keepdims=True)
    return (x * lax.rsqrt(ms + EPS) * g) * (1.0 + scale) + shift


def _adaln_kernel(c_ref, w_ref, b_ref, o_ref):
    c = c_ref[...]
    s = c * _sigmoid(c)
    o_ref[0] = _dot(s.astype(BF16), w_ref[0].astype(BF16)) + b_ref[0]


def _adaln(cvec, w_ada, b_ada):
    tn = 1536
    n = 6 * D_MODEL
    return pl.pallas_call(
        _adaln_kernel,
        out_shape=jax.ShapeDtypeStruct((DEPTH, MOD_ROWS, n), F32),
        grid=(DEPTH, n // tn),
        in_specs=[
            pl.BlockSpec((MOD_ROWS, D_MODEL), lambda l, j: (0, 0)),
            pl.BlockSpec((1, D_MODEL, tn), lambda l, j: (l, 0, j)),
            pl.BlockSpec((1, 1, tn), lambda l, j: (l, 0, j)),
        ],
        out_specs=pl.BlockSpec((1, MOD_ROWS, tn), lambda l, j: (l, 0, j)),
        compiler_params=_cparams(("arbitrary", "arbitrary")),
        name="adaln",
    )(cvec, w_ada, b_ada.reshape(DEPTH, 1, n))


def _rope(p, cos, sin_next, sin_prev):
    nxt = pltpu.roll(p, LANE - 16, 1)
    prv = pltpu.roll(p, 16, 1)
    return p * cos + nxt * sin_next + prv * sin_prev


def _inproj_kernel(*refs, rope):
    if rope:
        (x_ref, g_ref, mod_ref, w_ref, cos_ref, sn_ref, sp_ref, p_ref, ub_ref) = refs
    else:
        (x_ref, g_ref, mod_ref, w_ref, p_ref, ub_ref, dk_ref, dv_ref, wk_ref, wv_ref) = refs
    h = _modnorm(x_ref[...], g_ref[...], mod_ref[0, 0:1, :], mod_ref[0, 1:2, :])
    hb = h.astype(BF16)
    if rope:
        cos, sn, sp = cos_ref[...], sn_ref[...], sp_ref[...]
    lane = lax.broadcasted_iota(jnp.int32, (1, LANE), 1)
    for c in range(IN_W2 // 512):
        pc = _dot(hb, w_ref[:, c * 512:(c + 1) * 512])
        blocks = [pc[:, j * LANE:(j + 1) * LANE] for j in range(4)]
        if c == 5:
            ub_ref[...] = pc
            continue
        if rope:
            n_rot = {0: 4, 1: 4, 2: 0, 3: 4, 4: 2}[c]
            blocks = [_rope(b, cos, sn, sp) if j < n_rot else b for j, b in enumerate(blocks)]
        elif c == 1:
            dk_ref[...] = pc
        elif c == 2:
            dv_ref[...] = pc
        elif c == 4:
            wk_ref[...] = jnp.where(lane < DHC, blocks[0], blocks[1])
            wv_ref[...] = jnp.where(lane < DHC, blocks[2], blocks[3])
        for j in range(4):
            p_ref[:, (4 * c + j) * LANE:(4 * c + j + 1) * LANE] = blocks[j].astype(BF16)


def _inproj(x, g, mods, w, tm, rows_per_mod, rope_tabs=None):
    m = x.shape[0]
    rope = rope_tabs is not None
    mod_map = (lambda i: (i // (rows_per_mod // tm), 0, 0)) if rows_per_mod else (lambda i: (0, 0, 0))
    in_specs = [
        pl.BlockSpec((tm, D_MODEL), lambda i: (i, 0)),
        pl.BlockSpec((1, D_MODEL), lambda i: (0, 0)),
        pl.BlockSpec((1, 6, D_MODEL), mod_map),
        pl.BlockSpec((D_MODEL, IN_W2), lambda i: (0, 0)),
    ]
    args = [x, g, mods, w]
    out_shape = [jax.ShapeDtypeStruct((m, P_W), BF16), jax.ShapeDtypeStruct((m, POOL_W), F32)]
    out_specs = [pl.BlockSpec((tm, P_W), lambda i: (i, 0)), pl.BlockSpec((tm, POOL_W), lambda i: (i, 0))]
    if rope:
        t = rope_tabs[0].shape[0]
        for tab in rope_tabs:
            in_specs.append(pl.BlockSpec((tm, LANE), lambda i: (i % (t // tm), 0)))
            args.append(tab)
    else:
        for wdt in (512, 512, LANE, LANE):
            out_shape.append(jax.ShapeDtypeStruct((m, wdt), F32))
            out_specs.append(pl.BlockSpec((tm, wdt), lambda i: (i, 0)))
    return pl.pallas_call(
        functools.partial(_inproj_kernel, rope=rope),
        out_shape=out_shape,
        grid=(m // tm,),
        in_specs=in_specs,
        out_specs=out_specs,
        compiler_params=_cparams(("arbitrary",)),
        name="inproj_rope" if rope else "inproj",
    )(*args)


def _diff_lambda(lam_ref, lam_init):
    lf = lam_ref[...]
    s1 = jnp.sum(lf[0:1] * lf[1:2], axis=-1, keepdims=True)
    s2 = jnp.sum(lf[2:3] * lf[3:4], axis=-1, keepdims=True)
    return jnp.exp(s1) - jnp.exp(s2) + lam_init


def _diff_scores(q, k):
    lane = lax.broadcasted_iota(jnp.int32, (1, LANE), 1)
    zero = jnp.zeros_like(q)
    q2 = jnp.concatenate([jnp.where(lane < DHA, q, zero), jnp.where(lane < DHA, zero, q)], axis=0)
    return _dot_nt(q2, k)


def _diff_finish(s, v, lam, g, lam_init):
    tq = s.shape[0] // 2
    e = jnp.exp2(s - jnp.max(s, axis=-1, keepdims=True))
    inv = 1.0 / jnp.sum(e, axis=-1, keepdims=True)
    o2 = _dot(e.astype(BF16), v) * inv
    o = o2[:tq] - lam * o2[tq:]
    ms = jnp.mean(o * o, axis=-1, keepdims=True)
    return ((o * lax.rsqrt(ms + EPS) * g) * (1.0 - lam_init)).astype(BF16)


def _diff_tile(q, k, v, lam, g, lam_init):
    return _diff_finish(_diff_scores(q, k), v, lam, g, lam_init)


def _diff_ctx_kernel(q_ref, k_ref, v_ref, lam_ref, g_ref, o_ref, *, lam_init):
    lam = _diff_lambda(lam_ref, lam_init)
    g = g_ref[...]
    for h in range(HA):
        sl = slice(h * LANE, (h + 1) * LANE)
        o_ref[:, sl] = _diff_tile(q_ref[:, sl], k_ref[:, sl], v_ref[:, sl], lam, g, lam_init)


def _diff_ctx(p, lam_p, subln_g, lam_init, nb, s):
    w = HA * LANE
    return pl.pallas_call(
        functools.partial(_diff_ctx_kernel, lam_init=lam_init),
        out_shape=jax.ShapeDtypeStruct((nb * s, w), BF16),
        grid=(nb,),
        in_specs=[
            pl.BlockSpec((s, w), lambda b: (b, 0)),
            pl.BlockSpec((s, w), lambda b: (b, 1)),
            pl.BlockSpec((s, w), lambda b: (b, 2)),
            pl.BlockSpec((4, DHA), lambda b: (0, 0)),
            pl.BlockSpec((1, LANE), lambda b: (0, 0)),
        ],
        out_specs=pl.BlockSpec((s, w), lambda b: (b, 0)),
        compiler_params=_cparams(("arbitrary",)),
        name="diff_ctx",
    )(p, p, p, lam_p, subln_g)


def _diff_lat_kernel(q_ref, kn_ref, vn_ref, kc_ref, vc_ref, lam_ref, g_ref, o_ref,
                     k_all, v_all, s_even, s_odd, *, lam_init, past, tq):
    k_all[0:past, :] = kc_ref[0, 0].astype(BF16)
    v_all[0:past, :] = vc_ref[0, 0].astype(BF16)
    k_all[past:, :] = kn_ref[...]
    v_all[past:, :] = vn_ref[...]
    lam = _diff_lambda(lam_ref, lam_init)
    g = g_ref[...]
    n_tiles = q_ref.shape[0] // tq

    def rows(i):
        return pl.ds(pl.multiple_of(i * tq, tq), tq)

    s_even[...] = _diff_scores(q_ref[rows(0), :], k_all[...])

    def body(j, carry):
        i = 2 * j
        s_odd[...] = _diff_scores(q_ref[rows(i + 1), :], k_all[...])
        o_ref[rows(i), :] = _diff_finish(s_even[...], v_all[...], lam, g, lam_init)
        s_even[...] = _diff_scores(q_ref[rows(jnp.minimum(i + 2, n_tiles - 1)), :], k_all[...])
        o_ref[rows(i + 1), :] = _diff_finish(s_odd[...], v_all[...], lam, g, lam_init)
        return carry

    lax.fori_loop(0, n_tiles // 2, body, 0)


def _diff_lat(p, cache_k, cache_v, l, lam_p, subln_g, lam_init, nb, t, tq):
    past = cache_k.shape[2]
    return pl.pallas_call(
        functools.partial(_diff_lat_kernel, lam_init=lam_init, past=past, tq=tq),
        out_shape=jax.ShapeDtypeStruct((nb * t, HA * LANE), BF16),
        grid=(nb, HA),
        in_specs=[
            pl.BlockSpec((t, LANE), lambda b, h: (b, h)),
            pl.BlockSpec((t, LANE), lambda b, h: (b, 4 + h)),
            pl.BlockSpec((t, LANE), lambda b, h: (b, 8 + h)),
            pl.BlockSpec((1, 1, past, LANE), lambda b, h: (b, l, 0, h)),
            pl.BlockSpec((1, 1, past, LANE), lambda b, h: (b, l, 0, h)),
            pl.BlockSpec((4, DHA), lambda b, h: (0, 0)),
            pl.BlockSpec((1, LANE), lambda b, h: (0, 0)),
        ],
        out_specs=pl.BlockSpec((t, LANE), lambda b, h: (b, h)),
        scratch_shapes=[pltpu.VMEM((past + t, LANE), BF16), pltpu.VMEM((past + t, LANE), BF16),
                        pltpu.VMEM((2 * tq, past + t), F32), pltpu.VMEM((2 * tq, past + t), F32)],
        compiler_params=_cparams(("arbitrary", "arbitrary")),
        name="diff_lat",
    )(p, p, p, cache_k, cache_v, lam_p, subln_g)


POOL_PAD = 16


def _pool_kernel(u_ref, o_ref, pad_ref, *, t):
    tp = t + POOL_PAD
    pos = lax.broadcasted_iota(jnp.int32, (t, 1), 0)
    for gi, w in enumerate(POOL_WINDOWS):
        sl = slice(gi * POOL_GC, (gi + 1) * POOL_GC)
        u = u_ref[0, :, sl]
        pad_ref[0:t, :] = u
        pad_ref[t:tp, :] = jnp.zeros((POOL_PAD, POOL_GC), F32)
        acc = pad_ref[...]
        step = 1
        while step < w:
            acc = acc + pltpu.roll(acc, step, 0)
            step *= 2
        ahead = w - w // 2 - 1
        if ahead:
            acc = pltpu.roll(acc, tp - ahead, 0)
        lo = jnp.maximum(pos - w // 2, 0)
        hi = jnp.minimum(pos + (w - w // 2), t)
        mean = acc[0:t] / (hi - lo).astype(F32)
        o_ref[0, :, sl] = (mean - u).astype(BF16)


def _pool(ub, nb, t):
    return pl.pallas_call(
        functools.partial(_pool_kernel, t=t),
        out_shape=jax.ShapeDtypeStruct((nb, t, POOL_W), BF16),
        grid=(nb,),
        in_specs=[pl.BlockSpec((1, t, POOL_W), lambda b: (b, 0, 0))],
        out_specs=pl.BlockSpec((1, t, POOL_W), lambda b: (b, 0, 0)),
        scratch_shapes=[pltpu.VMEM((t + POOL_PAD, POOL_GC), F32)],
        compiler_params=_cparams(("arbitrary",)),
        name="pool",
    )(ub.reshape(nb, t, POOL_W)).reshape(nb * t, POOL_W)


def _stack_heads(q):
    lane = lax.broadcasted_iota(jnp.int32, (1, LANE), 1)
    zero = jnp.zeros_like(q[:, :LANE])
    parts = []
    for pair in range(2):
        qp = q[:, pair * LANE:(pair + 1) * LANE]
        parts.append(jnp.where(lane < DHC, qp, zero))
        parts.append(jnp.where(lane < DHC, zero, qp))
    return jnp.concatenate(parts, axis=0)


def _gqa_core(score_parts, masks, value_parts, sink_ref, g, tq):
    lane = lax.broadcasted_iota(jnp.int32, (1, LANE), 1)
    e_parts = [[] for _ in score_parts]
    invs = []
    for r in range(REPC):
        rows = slice(r * tq, (r + 1) * tq)
        sink = sink_ref[g * REPC + r] * LOG2E
        parts = [s[rows] if mk is None else jnp.where(mk, s[rows], NEG_INF)
                 for s, mk in zip(score_parts, masks)]
        m = jnp.maximum(functools.reduce(jnp.maximum, [jnp.max(s, axis=-1, keepdims=True) for s in parts]), sink)
        es = [jnp.exp2(s - m) for s in parts]
        den = functools.reduce(lambda a, b: a + b, [jnp.sum(e, axis=-1, keepdims=True) for e in es])
        invs.append(1.0 / (den + jnp.exp2(sink - m)))
        for lst, e in zip(e_parts, es):
            lst.append(e.astype(BF16))
    o = functools.reduce(lambda a, b: a + b,
                         [_dot(jnp.concatenate(lst, axis=0), v) for lst, v in zip(e_parts, value_parts)])
    o = o * jnp.concatenate(invs, axis=0)
    outs = [o[r * tq:(r + 1) * tq] for r in range(REPC)]
    return jnp.concatenate([jnp.where(lane < DHC, outs[0], outs[1]),
                            jnp.where(lane < DHC, outs[2], outs[3])], axis=1).astype(BF16)


def _gqa_ctx_kernel(sink_ref, q_ref, k_ref, v_ref, o_ref, *, s):
    for g in range(KVC):
        q4 = _stack_heads(q_ref[:, g * 2 * LANE:(g + 1) * 2 * LANE])
        sc = _dot_nt(q4, k_ref[:, g * LANE:(g + 1) * LANE])
        o_ref[:, g * 2 * LANE:(g + 1) * 2 * LANE] = _gqa_core(
            [sc], [None], [v_ref[:, g * LANE:(g + 1) * LANE]], sink_ref, g, s)


def _gqa_ctx(p, sink, nb, s):
    w = HC * DHC
    return pl.pallas_call(
        functools.partial(_gqa_ctx_kernel, s=s),
        out_shape=jax.ShapeDtypeStruct((nb * s, w), BF16),
        grid=(nb,),
        in_specs=[
            pl.BlockSpec(memory_space=pltpu.SMEM),
            pl.BlockSpec((s, w), lambda b: (b, 3)),
            pl.BlockSpec((s, KVC * LANE), lambda b: (b, 8)),
            pl.BlockSpec((s, KVC * LANE), lambda b: (b, 9)),
        ],
        out_specs=pl.BlockSpec((s, w), lambda b: (b, 0)),
        compiler_params=_cparams(("arbitrary",)),
        name="gqa_ctx",
    )(sink, p, p, p)


def _gqa_lat_kernel(sink_ref, q_ref, kn_ref, vn_ref, kc_ref, vc_ref, o_ref, kc_dup, vc_dup,
                    sc_even, sw_even, sc_odd, sw_odd, *, t):
    g = pl.program_id(1)
    lane = lax.broadcasted_iota(jnp.int32, (1, LANE), 1)
    keep = jnp.where(lane < DHC, 0, 1) == g
    for src, dst in ((kc_ref, kc_dup), (vc_ref, vc_dup)):
        blk = src[0, 0]
        dst[...] = jnp.where(keep, blk, pltpu.roll(blk, DHC, 1)).astype(BF16)
    span = QBLK + 2 * WINDOW
    n_blocks = t // QBLK

    def rows(i):
        return pl.ds(pl.multiple_of(i * QBLK, QBLK), QBLK)

    def win_start(i):
        return pl.multiple_of(jnp.clip((i - 1) * QBLK, 0, t - span), QBLK)

    def scores(i, sc_ref, sw_ref):
        q4 = _stack_heads(q_ref[rows(i), :])
        sc_ref[...] = _dot_nt(q4, kc_dup[...])
        sw_ref[...] = _dot_nt(q4, kn_ref[pl.ds(win_start(i), span), :])

    def finish(i, sc_ref, sw_ref):
        w0 = win_start(i)
        qpos = i * QBLK + lax.broadcasted_iota(jnp.int32, (QBLK, span), 0)
        kpos = w0 + lax.broadcasted_iota(jnp.int32, (QBLK, span), 1)
        band = jnp.abs(qpos - kpos) <= WINDOW
        o_ref[rows(i), :] = _gqa_core([sc_ref[...], sw_ref[...]], [None, band],
                                      [vc_dup[...], vn_ref[pl.ds(w0, span), :]], sink_ref, g, QBLK)

    scores(0, sc_even, sw_even)

    def body(j, carry):
        i = 2 * j
        scores(i + 1, sc_odd, sw_odd)
        finish(i, sc_even, sw_even)
        scores(jnp.minimum(i + 2, n_blocks - 1), sc_even, sw_even)
        finish(i + 1, sc_odd, sw_odd)
        return carry

    lax.fori_loop(0, n_blocks // 2, body, 0)


def _gqa_lat(p, cache_k, cache_v, l, sink, nb, t):
    past = cache_k.shape[2]
    return pl.pallas_call(
        functools.partial(_gqa_lat_kernel, t=t),
        out_shape=jax.ShapeDtypeStruct((nb * t, HC * DHC), BF16),
        grid=(nb, KVC),
        in_specs=[
            pl.BlockSpec(memory_space=pltpu.SMEM),
            pl.BlockSpec((t, 2 * LANE), lambda b, g: (b, 6 + g)),
            pl.BlockSpec((t, LANE), lambda b, g: (b, 16 + g)),
            pl.BlockSpec((t, LANE), lambda b, g: (b, 18 + g)),
            pl.BlockSpec((1, 1, past, LANE), lambda b, g: (b, l, 0, 0)),
            pl.BlockSpec((1, 1, past, LANE), lambda b, g: (b, l, 0, 0)),
        ],
        out_specs=pl.BlockSpec((t, 2 * LANE), lambda b, g: (b, g)),
        scratch_shapes=[pltpu.VMEM((past, LANE), BF16), pltpu.VMEM((past, LANE), BF16)]
        + [pltpu.VMEM((REPC * QBLK, past), F32), pltpu.VMEM((REPC * QBLK, QBLK + 2 * WINDOW), F32)] * 2,
        compiler_params=_cparams(("arbitrary", "arbitrary")),
        name="gqa_lat",
    )(sink, p, p, p, cache_k, cache_v)


def _merge_kernel(*refs, router):
    (x_ref, oa_ref, pb_ref, oc_ref, g1_ref, g2_ref, mod_ref, wgate_ref, wpool_ref, ps_ref,
     wbr_ref, wout_ref) = refs[:12]
    if router:
        wr_hi_ref, wr_lo_ref, xo_ref, h2_ref, lg_ref = refs[12:]
    else:
        xo_ref, h2_ref = refs[12:]
    x = x_ref[...]
    h = _modnorm(x, g1_ref[...], mod_ref[0, 0:1, :], mod_ref[0, 1:2, :])
    hb = h.astype(BF16)
    pooled = pb_ref[...]
    yb = jnp.concatenate(
        [_dot(pooled[:, j * POOL_GC:(j + 1) * POOL_GC], wpool_ref[j]) for j in range(len(POOL_WINDOWS))],
        axis=1) * ps_ref[...]
    branches = (oa_ref[...], yb.astype(BF16), oc_ref[...])
    acc = None
    for n in range(N_BRANCH):
        gate = _sigmoid(_dot(hb, wgate_ref[:, n * D_MODEL:(n + 1) * D_MODEL]))
        term = gate * _dot(branches[n], wbr_ref[n])
        acc = term if acc is None else acc + term
    xn = x + mod_ref[0, 2:3, :] * _dot(acc.astype(BF16), wout_ref[...])
    xo_ref[...] = xn
    h2 = _modnorm(xn, g2_ref[...], mod_ref[0, 3:4, :], mod_ref[0, 4:5, :])
    h2_hi = h2.astype(BF16)
    h2_ref[...] = h2_hi
    if router:
        h2_lo = (h2 - h2_hi.astype(F32)).astype(BF16)
        w_hi, w_lo = wr_hi_ref[...], wr_lo_ref[...]
        lg_ref[...] = (_dot(h2_hi, w_hi) + (_dot(h2_hi, w_lo) + _dot(h2_lo, w_hi))) + _dot(h2_lo, w_lo)


def _merge(x, oa, pooled, oc, g1, g2, mods, wgate, wpool, pscale, wbr, wout, tm, rows_per_mod, wr=None):
    m = x.shape[0]
    router = wr is not None
    mod_map = (lambda i: (i // (rows_per_mod // tm), 0, 0)) if rows_per_mod else (lambda i: (0, 0, 0))
    row = lambda w: pl.BlockSpec((tm, w), lambda i: (i, 0))
    full = lambda a: pl.BlockSpec(a.shape, lambda i: (0,) * a.ndim)
    in_specs = [row(D_MODEL), row(BRANCH_W), row(BRANCH_W), row(BRANCH_W), full(g1), full(g2),
                pl.BlockSpec((1, 6, D_MODEL), mod_map), full(wgate), full(wpool), full(pscale),
                full(wbr), full(wout)]
    args = [x, oa, pooled, oc, g1, g2, mods, wgate, wpool, pscale, wbr, wout]
    out_shape = [jax.ShapeDtypeStruct((m, D_MODEL), F32), jax.ShapeDtypeStruct((m, D_MODEL), BF16)]
    out_specs = [row(D_MODEL), row(D_MODEL)]
    if router:
        in_specs += [full(wr[0]), full(wr[1])]
        args += [wr[0], wr[1]]
        out_shape.append(jax.ShapeDtypeStruct((m, LANE), F32))
        out_specs.append(row(LANE))
    return pl.pallas_call(
        functools.partial(_merge_kernel, router=router),
        out_shape=out_shape,
        grid=(m // tm,),
        in_specs=in_specs,
        out_specs=out_specs,
        compiler_params=_cparams(("arbitrary",)),
        name="merge",
    )(*args)


def _top2_gates(logits):
    lane = lax.broadcasted_iota(jnp.int32, logits.shape, 1).astype(F32)
    neg = -jnp.inf
    l1 = jnp.where(lane < N_EXP, logits, neg)
    m1 = jnp.max(l1, axis=-1, keepdims=True)
    i1 = jnp.min(jnp.where(l1 == m1, lane, float(LANE)), axis=-1, keepdims=True)
    l2 = jnp.where(lane == i1, neg, l1)
    m2 = jnp.max(l2, axis=-1, keepdims=True)
    i2 = jnp.min(jnp.where(l2 == m2, lane, float(LANE)), axis=-1, keepdims=True)
    e2 = jnp.exp(m2 - m1)
    den = 1.0 + e2
    return jnp.where(lane == i1, 1.0 / den, 0.0) + jnp.where(lane == i2, e2 / den, 0.0)


def _ffn_kernel(*refs, moe, final):
    refs = list(refs)
    h_ref, x_ref, mod_ref = refs[:3]
    k = 3
    if moe:
        lg_ref = refs[k]
        k += 1
    wg_ref, wu_ref, wd_ref = refs[k:k + 3]
    k += 3
    if final:
        fg_ref = refs[k]
        k += 1
    o_ref, acc_ref = refs[k], refs[k + 1]
    if moe:
        gate_ref = refs[k + 2]
    e = pl.program_id(1)

    @pl.when(e == 0)
    def _():
        acc_ref[...] = jnp.zeros_like(acc_ref)
        if moe:
            gate_ref[...] = _top2_gates(lg_ref[...])

    h = h_ref[...]
    a = _dot(h, wg_ref[...])
    u = _dot(h, wu_ref[...])
    y = _dot(((a * _sigmoid(a)) * u).astype(BF16), wd_ref[...])
    if moe:
        lane = lax.broadcasted_iota(jnp.int32, (1, LANE), 1)
        ge = jnp.sum(jnp.where(lane == e, gate_ref[...], 0.0), axis=-1, keepdims=True)
        y = ge * y
    acc_ref[...] += y

    @pl.when(e == pl.num_programs(1) - 1)
    def _():
        xn = x_ref[...] + mod_ref[0, 5:6, :] * acc_ref[...]
        if final:
            ms = jnp.mean(xn * xn, axis=-1, keepdims=True)
            xn = xn * lax.rsqrt(ms + EPS) * fg_ref[...]
        o_ref[...] = xn


def _ffn(h2, x, mods, wg, wu, wd, tm, rows_per_mod, logits=None, final_g=None):
    m = x.shape[0]
    moe = logits is not None
    final = final_g is not None
    mod_map = (lambda i, e: (i // (rows_per_mod // tm), 0, 0)) if rows_per_mod else (lambda i, e: (0, 0, 0))
    row = lambda w: pl.BlockSpec((tm, w), lambda i, e: (i, 0))
    in_specs = [row(D_MODEL), row(D_MODEL), pl.BlockSpec((1, 6, D_MODEL), mod_map)]
    args = [h2, x, mods]
    if moe:
        in_specs.append(row(LANE))
        args.append(logits)
        n_chunks = N_EXP
        in_specs += [pl.BlockSpec((None, D_MODEL, D_FF_E), lambda i, e: (e, 0, 0)),
                     pl.BlockSpec((None, D_MODEL, D_FF_E), lambda i, e: (e, 0, 0)),
                     pl.BlockSpec((None, D_FF_E, D_MODEL), lambda i, e: (e, 0, 0))]
    else:
        n_chunks = wg.shape[1] // D_FF_E
        in_specs += [pl.BlockSpec((D_MODEL, D_FF_E), lambda i, e: (0, e)),
                     pl.BlockSpec((D_MODEL, D_FF_E), lambda i, e: (0, e)),
                     pl.BlockSpec((D_FF_E, D_MODEL), lambda i, e: (e, 0))]
    args += [wg, wu, wd]
    if final:
        in_specs.append(pl.BlockSpec((1, D_MODEL), lambda i, e: (0, 0)))
        args.append(final_g)
    scratch = [pltpu.VMEM((tm, D_MODEL), F32)]
    if moe:
        scratch.append(pltpu.VMEM((tm, LANE), F32))
    return pl.pallas_call(
        functools.partial(_ffn_kernel, moe=moe, final=final),
        out_shape=jax.ShapeDtypeStruct((m, D_MODEL), F32),
        grid=(m // tm, n_chunks),
        in_specs=in_specs,
        out_specs=row(D_MODEL),
        scratch_shapes=scratch,
        compiler_params=_cparams(("arbitrary", "arbitrary")),
        name="moe" if moe else "ffn",
    )(*args)


def _prep_w_in(w):
    qa, ka, va, ub, qc = (w[:, 0:512], w[:, 512:1024], w[:, 1024:1536], w[:, 1536:2048], w[:, 2048:2560])
    kc, vc = w[:, 2560:2688], w[:, 2688:2816]
    dup = lambda a: jnp.concatenate([a[:, 0:64], a[:, 0:64], a[:, 64:128], a[:, 64:128]], axis=1)
    scale = LOG2E * DHA ** -0.5
    return jnp.concatenate([qa * scale, ka, va, qc * scale, dup(kc), dup(vc), ub], axis=1).astype(BF16)


def _rope_tables(t):
    rows = t // GRID_W
    row = jnp.repeat(jnp.arange(rows, dtype=F32), GRID_W)
    col = jnp.tile(jnp.arange(GRID_W, dtype=F32), rows)
    inv = ROPE_BASE ** (-jnp.arange(0, ROT_HALF, 2, dtype=F32) / ROT_HALF)
    ang_r, ang_c = row[:, None] * inv, col[:, None] * inv
    zero = jnp.zeros_like(ang_r)
    seg = lambda a, b: jnp.concatenate([a, b], axis=1)
    cos = jnp.concatenate([seg(jnp.cos(ang_r), jnp.cos(ang_r)), seg(jnp.cos(ang_c), jnp.cos(ang_c))], axis=1)
    s_next = jnp.concatenate([seg(-jnp.sin(ang_r), zero), seg(-jnp.sin(ang_c), zero)], axis=1)
    s_prev = jnp.concatenate([seg(zero, jnp.sin(ang_r)), seg(zero, jnp.sin(ang_c))], axis=1)
    two = lambda a: jnp.concatenate([a, a], axis=1)
    return two(cos), two(s_next), two(s_prev)


def kernel(x_prompt, x_sample, cache_diff_k, cache_diff_v, cache_win_k, cache_win_v, c, c_ctx, w_ada, b_ada, norm1_g, norm2_g, w_in, w_gate, lam_p, subln_g, w_pool, pool_scale, sink, w_branch, w_out, w_ff_gate, w_ff_up, w_ff_down, w_router, w_ex_gate, w_ex_up, w_ex_down, final_g):
    bp, s, _ = x_prompt.shape
    bs, t, _ = x_sample.shape
    past = cache_diff_k.shape[2]

    cvec = jnp.concatenate([c_ctx[None, :], c, jnp.zeros((MOD_ROWS - 1 - bs, D_MODEL), F32)], axis=0)
    mods = _adaln(cvec, w_ada, b_ada).reshape(DEPTH, MOD_ROWS, 6, D_MODEL)
    rope_tabs = _rope_tables(t)

    ck_a = cache_diff_k.reshape(bs, DEPTH, past, HA * 2 * DHA)
    cv_a = cache_diff_v.reshape(bs, DEPTH, past, HA * 2 * DHA)
    ck_c = cache_win_k.reshape(bs, DEPTH, past, KVC * DHC)
    cv_c = cache_win_v.reshape(bs, DEPTH, past, KVC * DHC)

    xc = x_prompt.reshape(bp * s, D_MODEL)
    xl = x_sample.reshape(bs * t, D_MODEL)
    caches = [[], [], [], []]
    for l in range(DEPTH):
        last = l == DEPTH - 1
        lam_init = 0.8 - 0.6 * math.exp(-0.3 * l)
        w_in_l = _prep_w_in(w_in[l])
        wgate_l = w_gate[l].astype(BF16)
        wpool_l = w_pool[l].astype(BF16)
        wbr_l = w_branch[l].astype(BF16)
        wout_l = w_out[l].astype(BF16)
        g1 = norm1_g[l][None, :]
        g2 = norm2_g[l][None, :]
        sg = subln_g[l][None, :]
        ps = pool_scale[l][None, :]
        fg = final_g[None, :] if last else None
        i = l // 2
        if l % 2 == 0:
            wr = None
            wg, wu, wd = (w_ff_gate[i].astype(BF16), w_ff_up[i].astype(BF16), w_ff_down[i].astype(BF16))
        else:
            wr_f = jnp.pad(w_router[i], ((0, 0), (0, LANE - N_EXP)))
            wr_hi = wr_f.astype(BF16)
            wr = (wr_hi, (wr_f - wr_hi.astype(F32)).astype(BF16))
            wg, wu, wd = (w_ex_gate[i].astype(BF16), w_ex_up[i].astype(BF16), w_ex_down[i].astype(BF16))

        m_c = mods[l, 0:1]
        pc, ubc, dk, dv, wk, wv = _inproj(xc, g1, m_c, w_in_l, tm=256, rows_per_mod=0)
        for lst, a in zip(caches, (dk, dv, wk, wv)):
            lst.append(a)
        oa = _diff_ctx(pc, lam_p[l], sg, lam_init, bp, s)
        pooled = _pool(ubc, bp, s)
        oc = _gqa_ctx(pc, sink[l], bp, s)
        res = _merge(xc, oa, pooled, oc, g1, g2, m_c, wgate_l, wpool_l, ps, wbr_l, wout_l,
                     tm=256, rows_per_mod=0, wr=wr)
        xc = _ffn(res[1], res[0], m_c, wg, wu, wd, tm=512, rows_per_mod=0,
                  logits=res[2] if wr else None, final_g=fg)

        m_l = mods[l, 1:1 + bs]
        pl_, ubl = _inproj(xl, g1, m_l, w_in_l, tm=256, rows_per_mod=t, rope_tabs=rope_tabs)
        oa = _diff_lat(pl_, ck_a, cv_a, l, lam_p[l], sg, lam_init, bs, t, tq=128)
        pooled = _pool(ubl, bs, t)
        oc = _gqa_lat(pl_, ck_c, cv_c, l, sink[l], bs, t)
        res = _merge(xl, oa, pooled, oc, g1, g2, m_l, wgate_l, wpool_l, ps, wbr_l, wout_l,
                     tm=256, rows_per_mod=t, wr=wr)
        xl = _ffn(res[1], res[0], m_l, wg, wu, wd, tm=512, rows_per_mod=t,
                  logits=res[2] if wr else None, final_g=fg)

    y_prompt = xc.reshape(bp, s, D_MODEL)
    y_sample = xl.reshape(bs, t, D_MODEL)
    new_diff_k = jnp.stack(caches[0], axis=0).reshape(DEPTH, bp, s, HA, 2, DHA).swapaxes(0, 1)
    new_diff_v = jnp.stack(caches[1], axis=0).reshape(DEPTH, bp, s, HA, 2 * DHA).swapaxes(0, 1)
    new_win_k = jnp.stack(caches[2], axis=0).reshape(DEPTH, bp, s, KVC, DHC).swapaxes(0, 1)
    new_win_v = jnp.stack(caches[3], axis=0).reshape(DEPTH, bp, s, KVC, DHC).swapaxes(0, 1)
    return (y_prompt, y_sample, new_diff_k, new_diff_v, new_win_k, new_win_v)
```

```python
import functools
import math

import numpy as np
import jax
import jax.numpy as jnp
from jax import lax
from jax.experimental import pallas as pl
from jax.experimental.pallas import tpu as pltpu

F32 = jnp.float32
BF16 = jnp.bfloat16

D_MODEL = 1024
DEPTH = 2
GRID_W = 64
HA = 4
DHA = 64
HC = 8
KVC = 2
REPC = HC // KVC
DHC = 64
WINDOW = 128
QBLK = 128
POOL_WINDOWS = (2, 4, 8, 16)
POOL_GC = 128
POOL_W = 512
BRANCH_W = 512
N_BRANCH = 3
ROPE_BASE = 10000.0
ROT_HALF = 32
N_EXP = 8
TOP_K = 2
D_FF_E = 1408
EPS = 1e-6
NEG_INF = -1e30
LOG2E = math.log2(math.e)

LANE = 128
MOD_ROWS = 16
P_W = 20 * LANE
IN_W2 = 24 * LANE
VMEM_LIMIT = 56 * 1024 * 1024


def _cparams(sem):
    return pltpu.CompilerParams(dimension_semantics=sem, vmem_limit_bytes=VMEM_LIMIT)


def _dot(a, b):
    return jnp.dot(a, b, preferred_element_type=F32)


def _dot_nt(a, b):
    return lax.dot_general(a, b, (((1,), (1,)), ((), ())), preferred_element_type=F32)


def _sigmoid(x):
    return 1.0 / (1.0 + jnp.exp(-x))


def _modnorm(x, g, shift, scale):
    ms = jnp.mean(x * x, axis=-1, keepdims=True)
    return (x * lax.rsqrt(ms + EPS) * g) * (1.0 + scale) + shift


def _adaln_kernel(c_ref, w_ref, b_ref, o_ref):
    c = c_ref[...]
    s = c * _sigmoid(c)
    o_ref[0] = _dot(s.astype(BF16), w_ref[0].astype(BF16)) + b_ref[0]


def _adaln(cvec, w_ada, b_ada):
    tn = 1536
    n = 6 * D_MODEL
    return pl.pallas_call(
        _adaln_kernel,
        out_shape=jax.ShapeDtypeStruct((DEPTH, MOD_ROWS, n), F32),
        grid=(DEPTH, n // tn),
        in_specs=[
            pl.BlockSpec((MOD_ROWS, D_MODEL), lambda l, j: (0, 0)),
            pl.BlockSpec((1, D_MODEL, tn), lambda l, j: (l, 0, j)),
            pl.BlockSpec((1, 1, tn), lambda l, j: (l, 0, j)),
        ],
        out_specs=pl.BlockSpec((1, MOD_ROWS, tn), lambda l, j: (l, 0, j)),
        compiler_params=_cparams(("arbitrary", "arbitrary")),
        name="adaln",
    )(cvec, w_ada, b_ada.reshape(DEPTH, 1, n))


def _rope(p, cos, sin_next, sin_prev):
    nxt = pltpu.roll(p, LANE - 16, 1)
    prv = pltpu.roll(p, 16, 1)
    return p * cos + nxt * sin_next + prv * sin_prev


def _inproj_kernel(*refs, rope):
    if rope:
        (x_ref, g_ref, mod_ref, w_ref, cos_ref, sn_ref, sp_ref, p_ref, ub_ref) = refs
    else:
        (x_ref, g_ref, mod_ref, w_ref, p_ref, ub_ref, dk_ref, dv_ref, wk_ref, wv_ref) = refs
    h = _modnorm(x_ref[...], g_ref[...], mod_ref[0, 0:1, :], mod_ref[0, 1:2, :])
    hb = h.astype(BF16)
    if rope:
        cos, sn, sp = cos_ref[...], sn_ref[...], sp_ref[...]
    lane = lax.broadcasted_iota(jnp.int32, (1, LANE), 1)
    for c in range(IN_W2 // 512):
        pc = _dot(hb, w_ref[:, c * 512:(c + 1) * 512])
        blocks = [pc[:, j * LANE:(j + 1) * LANE] for j in range(4)]
        if c == 5:
            ub_ref[...] = pc
            continue
        if rope:
            n_rot = {0: 4, 1: 4, 2: 0, 3: 4, 4: 2}[c]
            blocks = [_rope(b, cos, sn, sp) if j < n_rot else b for j, b in enumerate(blocks)]
        elif c == 1:
            dk_ref[...] = pc
        elif c == 2:
            dv_ref[...] = pc
        elif c == 4:
            wk_ref[...] = jnp.where(lane < DHC, blocks[0], blocks[1])
            wv_ref[...] = jnp.where(lane < DHC, blocks[2], blocks[3])
        for j in range(4):
            p_ref[:, (4 * c + j) * LANE:(4 * c + j + 1) * LANE] = blocks[j].astype(BF16)


def _inproj(x, g, mods, w, tm, rows_per_mod, rope_tabs=None):
    m = x.shape[0]
    rope = rope_tabs is not None
    mod_map = (lambda i: (i // (rows_per_mod // tm), 0, 0)) if rows_per_mod else (lambda i: (0, 0, 0))
    in_specs = [
        pl.BlockSpec((tm, D_MODEL), lambda i: (i, 0)),
        pl.BlockSpec((1, D_MODEL), lambda i: (0, 0)),
        pl.BlockSpec((1, 6, D_MODEL), mod_map),
        pl.BlockSpec((D_MODEL, IN_W2), lambda i: (0, 0)),
    ]
    args = [x, g, mods, w]
    out_shape = [jax.ShapeDtypeStruct((m, P_W), BF16), jax.ShapeDtypeStruct((m, POOL_W), F32)]
    out_specs = [pl.BlockSpec((tm, P_W), lambda i: (i, 0)), pl.BlockSpec((tm, POOL_W), lambda i: (i, 0))]
    if rope:
        t = rope_tabs[0].shape[0]
        for tab in rope_tabs:
            in_specs.append(pl.BlockSpec((tm, LANE), lambda i: (i % (t // tm), 0)))
            args.append(tab)
    else:
        for wdt in (512, 512, LANE, LANE):
            out_shape.append(jax.ShapeDtypeStruct((m, wdt), F32))
            out_specs.append(pl.BlockSpec((tm, wdt), lambda i: (i, 0)))
    return pl.pallas_call(
        functools.partial(_inproj_kernel, rope=rope),
        out_shape=out_shape,
        grid=(m // tm,),
        in_specs=in_specs,
        out_specs=out_specs,
        compiler_params=_cparams(("arbitrary",)),
        name="inproj_rope" if rope else "inproj",
    )(*args)


def _diff_lambda(lam_ref, lam_init):
    lf = lam_ref[...]
    s1 = jnp.sum(lf[0:1] * lf[1:2], axis=-1, keepdims=True)
    s2 = jnp.sum(lf[2:3] * lf[3:4], axis=-1, keepdims=True)
    return jnp.exp(s1) - jnp.exp(s2) + lam_init


def _diff_scores(q, k):
    lane = lax.broadcasted_iota(jnp.int32, (1, LANE), 1)
    zero = jnp.zeros_like(q)
    q2 = jnp.concatenate([jnp.where(lane < DHA, q, zero), jnp.where(lane < DHA, zero, q)], axis=0)
    return _dot_nt(q2, k)


def _diff_finish(s, v, lam, g, lam_init):
    tq = s.shape[0] // 2
    e = jnp.exp2(s - jnp.max(s, axis=-1, keepdims=True))
    inv = 1.0 / jnp.sum(e, axis=-1, keepdims=True)
    o2 = _dot(e.astype(BF16), v) * inv
    o = o2[:tq] - lam * o2[tq:]
    ms = jnp.mean(o * o, axis=-1, keepdims=True)
    return ((o * lax.rsqrt(ms + EPS) * g) * (1.0 - lam_init)).astype(BF16)


def _diff_tile(q, k, v, lam, g, lam_init):
    return _diff_finish(_diff_scores(q, k), v, lam, g, lam_init)


def _diff_ctx_kernel(q_ref, k_ref, v_ref, lam_ref, g_ref, o_ref, *, lam_init):
    lam = _diff_lambda(lam_ref, lam_init)
    g = g_ref[...]
    for h in range(HA):
        sl = slice(h * LANE, (h + 1) * LANE)
        o_ref[:, sl] = _diff_tile(q_ref[:, sl], k_ref[:, sl], v_ref[:, sl], lam, g, lam_init)


def _diff_ctx(p, lam_p, subln_g, lam_init, nb, s):
    w = HA * LANE
    return pl.pallas_call(
        functools.partial(_diff_ctx_kernel, lam_init=lam_init),
        out_shape=jax.ShapeDtypeStruct((nb * s, w), BF16),
        grid=(nb,),
        in_specs=[
            pl.BlockSpec((s, w), lambda b: (b, 0)),
            pl.BlockSpec((s, w), lambda b: (b, 1)),
            pl.BlockSpec((s, w), lambda b: (b, 2)),
            pl.BlockSpec((4, DHA), lambda b: (0, 0)),
            pl.BlockSpec((1, LANE), lambda b: (0, 0)),
        ],
        out_specs=pl.BlockSpec((s, w), lambda b: (b, 0)),
        compiler_params=_cparams(("arbitrary",)),
        name="diff_ctx",
    )(p, p, p, lam_p, subln_g)


def _diff_lat_kernel(q_ref, kn_ref, vn_ref, kc_ref, vc_ref, lam_ref, g_ref, o_ref,
                     k_all, v_all, s_even, s_odd, *, lam_init, past, tq):
    k_all[0:past, :] = kc_ref[0, 0].astype(BF16)
    v_all[0:past, :] = vc_ref[0, 0].astype(BF16)
    k_all[past:, :] = kn_ref[...]
    v_all[past:, :] = vn_ref[...]
    lam = _diff_lambda(lam_ref, lam_init)
    g = g_ref[...]
    n_tiles = q_ref.shape[0] // tq

    def rows(i):
        return pl.ds(pl.multiple_of(i * tq, tq), tq)

    s_even[...] = _diff_scores(q_ref[rows(0), :], k_all[...])

    def body(j, carry):
        i = 2 * j
        s_odd[...] = _diff_scores(q_ref[rows(i + 1), :], k_all[...])
        o_ref[rows(i), :] = _diff_finish(s_even[...], v_all[...], lam, g, lam_init)
        s_even[...] = _diff_scores(q_ref[rows(jnp.minimum(i + 2, n_tiles - 1)), :], k_all[...])
        o_ref[rows(i + 1), :] = _diff_finish(s_odd[...], v_all[...], lam, g, lam_init)
        return carry

    lax.fori_loop(0, n_tiles // 2, body, 0)


def _diff_lat(p, cache_k, cache_v, l, lam_p, subln_g, lam_init, nb, t, tq):
    past = cache_k.shape[2]
    return pl.pallas_call(
        functools.partial(_diff_lat_kernel, lam_init=lam_init, past=past, tq=tq),
        out_shape=jax.ShapeDtypeStruct((nb * t, HA * LANE), BF16),
        grid=(nb, HA),
        in_specs=[
            pl.BlockSpec((t, LANE), lambda b, h: (b, h)),
            pl.BlockSpec((t, LANE), lambda b, h: (b, 4 + h)),
            pl.BlockSpec((t, LANE), lambda b, h: (b, 8 + h)),
            pl.BlockSpec((1, 1, past, LANE), lambda b, h: (b, l, 0, h)),
            pl.BlockSpec((1, 1, past, LANE), lambda b, h: (b, l, 0, h)),
            pl.BlockSpec((4, DHA), lambda b, h: (0, 0)),
            pl.BlockSpec((1, LANE), lambda b, h: (0, 0)),
        ],
        out_specs=pl.BlockSpec((t, LANE), lambda b, h: (b, h)),
        scratch_shapes=[pltpu.VMEM((past + t, LANE), BF16), pltpu.VMEM((past + t, LANE), BF16),
                        pltpu.VMEM((2 * tq, past + t), F32), pltpu.VMEM((2 * tq, past + t), F32)],
        compiler_params=_cparams(("arbitrary", "arbitrary")),
        name="diff_lat",
    )(p, p, p, cache_k, cache_v, lam_p, subln_g)


POOL_PAD = 16


def _pool_kernel(u_ref, o_ref, pad_ref, *, t):
    tp = t + POOL_PAD
    pos = lax.broadcasted_iota(jnp.int32, (t, 1), 0)
    for gi, w in enumerate(POOL_WINDOWS):
        sl = slice(gi * POOL_GC, (gi + 1) * POOL_GC)
        u = u_ref[0, :, sl]
        pad_ref[0:t, :] = u
        pad_ref[t:tp, :] = jnp.zeros((POOL_PAD, POOL_GC), F32)
        acc = pad_ref[...]
        step = 1
        while step < w:
            acc = acc + pltpu.roll(acc, step, 0)
            step *= 2
        ahead = w - w // 2 - 1
        if ahead:
            acc = pltpu.roll(acc, tp - ahead, 0)
        lo = jnp.maximum(pos - w // 2, 0)
        hi = jnp.minimum(pos + (w - w // 2), t)
        mean = acc[0:t] / (hi - lo).astype(F32)
        o_ref[0, :, sl] = (mean - u).astype(BF16)


def _pool(ub, nb, t):
    return pl.pallas_call(
        functools.partial(_pool_kernel, t=t),
        out_shape=jax.ShapeDtypeStruct((nb, t, POOL_W), BF16),
        grid=(nb,),
        in_specs=[pl.BlockSpec((1, t, POOL_W), lambda b: (b, 0, 0))],
        out_specs=pl.BlockSpec((1, t, POOL_W), lambda b: (b, 0, 0)),
        scratch_shapes=[pltpu.VMEM((t + POOL_PAD, POOL_GC), F32)],
        compiler_params=_cparams(("arbitrary",)),
        name="pool",
    )(ub.reshape(nb, t, POOL_W)).reshape(nb * t, POOL_W)


def _stack_heads(q):
    lane = lax.broadcasted_iota(jnp.int32, (1, LANE), 1)
    zero = jnp.zeros_like(q[:, :LANE])
    parts = []
    for pair in range(2):
        qp = q[:, pair * LANE:(pair + 1) * LANE]
        parts.append(jnp.where(lane < DHC, qp, zero))
        parts.append(jnp.where(lane < DHC, zero, qp))
    return jnp.concatenate(parts, axis=0)


def _gqa_core(score_parts, masks, value_parts, sink_ref, g, tq):
    lane = lax.broadcasted_iota(jnp.int32, (1, LANE), 1)
    e_parts = [[] for _ in score_parts]
    invs = []
    for r in range(REPC):
        rows = slice(r * tq, (r + 1) * tq)
        sink = sink_ref[g * REPC + r] * LOG2E
        parts = [s[rows] if mk is None else jnp.where(mk, s[rows], NEG_INF)
                 for s, mk in zip(score_parts, masks)]
        m = jnp.maximum(functools.reduce(jnp.maximum, [jnp.max(s, axis=-1, keepdims=True) for s in parts]), sink)
        es = [jnp.exp2(s - m) for s in parts]
        den = functools.reduce(lambda a, b: a + b, [jnp.sum(e, axis=-1, keepdims=True) for e in es])
        invs.append(1.0 / (den + jnp.exp2(sink - m)))
        for lst, e in zip(e_parts, es):
            lst.append(e.astype(BF16))
    o = functools.reduce(lambda a, b: a + b,
                         [_dot(jnp.concatenate(lst, axis=0), v) for lst, v in zip(e_parts, value_parts)])
    o = o * jnp.concatenate(invs, axis=0)
    outs = [o[r * tq:(r + 1) * tq] for r in range(REPC)]
    return jnp.concatenate([jnp.where(lane < DHC, outs[0], outs[1]),
                            jnp.where(lane < DHC, outs[2], outs[3])], axis=1).astype(BF16)


def _gqa_ctx_kernel(sink_ref, q_ref, k_ref, v_ref, o_ref, *, s):
    for g in range(KVC):
        q4 = _stack_heads(q_ref[:, g * 2 * LANE:(g + 1) * 2 * LANE])
        sc = _dot_nt(q4, k_ref[:, g * LANE:(g + 1) * LANE])
        o_ref[:, g * 2 * LANE:(g + 1) * 2 * LANE] = _gqa_core(
            [sc], [None], [v_ref[:, g * LANE:(g + 1) * LANE]], sink_ref, g, s)


def _gqa_ctx(p, sink, nb, s):
    w = HC * DHC
    return pl.pallas_call(
        functools.partial(_gqa_ctx_kernel, s=s),
        out_shape=jax.ShapeDtypeStruct((nb * s, w), BF16),
        grid=(nb,),
        in_specs=[
            pl.BlockSpec(memory_space=pltpu.SMEM),
            pl.BlockSpec((s, w), lambda b: (b, 3)),
            pl.BlockSpec((s, KVC * LANE), lambda b: (b, 8)),
            pl.BlockSpec((s, KVC * LANE), lambda b: (b, 9)),
        ],
        out_specs=pl.BlockSpec((s, w), lambda b: (b, 0)),
        compiler_params=_cparams(("arbitrary",)),
        name="gqa_ctx",
    )(sink, p, p, p)


def _gqa_lat_kernel(sink_ref, q_ref, kn_ref, vn_ref, kc_ref, vc_ref, o_ref, kc_dup, vc_dup,
                    sc_even, sw_even, sc_odd, sw_odd, *, t):
    g = pl.program_id(1)
    lane = lax.broadcasted_iota(jnp.int32, (1, LANE), 1)
    keep = jnp.where(lane < DHC, 0, 1) == g
    for src, dst in ((kc_ref, kc_dup), (vc_ref, vc_dup)):
        blk = src[0, 0]
        dst[...] = jnp.where(keep, blk, pltpu.roll(blk, DHC, 1)).astype(BF16)
    span = QBLK + 2 * WINDOW
    n_blocks = t // QBLK

    def rows(i):
        return pl.ds(pl.multiple_of(i * QBLK, QBLK), QBLK)

    def win_start(i):
        return pl.multiple_of(jnp.clip((i - 1) * QBLK, 0, t - span), QBLK)

    def scores(i, sc_ref, sw_ref):
        q4 = _stack_heads(q_ref[rows(i), :])
        sc_ref[...] = _dot_nt(q4, kc_dup[...])
        sw_ref[...] = _dot_nt(q4, kn_ref[pl.ds(win_start(i), span), :])

    def finish(i, sc_ref, sw_ref):
        w0 = win_start(i)
        qpos = i * QBLK + lax.broadcasted_iota(jnp.int32, (QBLK, span), 0)
        kpos = w0 + lax.broadcasted_iota(jnp.int32, (QBLK, span), 1)
        band = jnp.abs(qpos - kpos) <= WINDOW
        o_ref[rows(i), :] = _gqa_core([sc_ref[...], sw_ref[...]], [None, band],
                                      [vc_dup[...], vn_ref[pl.ds(w0, span), :]], sink_ref, g, QBLK)

    scores(0, sc_even, sw_even)

    def body(j, carry):
        i = 2 * j
        scores(i + 1, sc_odd, sw_odd)
        finish(i, sc_even, sw_even)
        scores(jnp.minimum(i + 2, n_blocks - 1), sc_even, sw_even)
        finish(i + 1, sc_odd, sw_odd)
        return carry

    lax.fori_loop(0, n_blocks // 2, body, 0)


def _gqa_lat(p, cache_k, cache_v, l, sink, nb, t):
    past = cache_k.shape[2]
    return pl.pallas_call(
        functools.partial(_gqa_lat_kernel, t=t),
        out_shape=jax.ShapeDtypeStruct((nb * t, HC * DHC), BF16),
        grid=(nb, KVC),
        in_specs=[
            pl.BlockSpec(memory_space=pltpu.SMEM),
            pl.BlockSpec((t, 2 * LANE), lambda b, g: (b, 6 + g)),
            pl.BlockSpec((t, LANE), lambda b, g: (b, 16 + g)),
            pl.BlockSpec((t, LANE), lambda b, g: (b, 18 + g)),
            pl.BlockSpec((1, 1, past, LANE), lambda b, g: (b, l, 0, 0)),
            pl.BlockSpec((1, 1, past, LANE), lambda b, g: (b, l, 0, 0)),
        ],
        out_specs=pl.BlockSpec((t, 2 * LANE), lambda b, g: (b, g)),
        scratch_shapes=[pltpu.VMEM((past, LANE), BF16), pltpu.VMEM((past, LANE), BF16)]
        + [pltpu.VMEM((REPC * QBLK, past), F32), pltpu.VMEM((REPC * QBLK, QBLK + 2 * WINDOW), F32)] * 2,
        compiler_params=_cparams(("arbitrary", "arbitrary")),
        name="gqa_lat",
    )(sink, p, p, p, cache_k, cache_v)


def _merge_kernel(*refs, router):
    (x_ref, oa_ref, pb_ref, oc_ref, g1_ref, g2_ref, mod_ref, wgate_ref, wpool_ref, ps_ref,
     wbr_ref, wout_ref) = refs[:12]
    if router:
        wr_hi_ref, wr_lo_ref, xo_ref, h2_ref, lg_ref = refs[12:]
    else:
        xo_ref, h2_ref = refs[12:]
    x = x_ref[...]
    h = _modnorm(x, g1_ref[...], mod_ref[0, 0:1, :], mod_ref[0, 1:2, :])
    hb = h.astype(BF16)
    pooled = pb_ref[...]
    yb = jnp.concatenate(
        [_dot(pooled[:, j * POOL_GC:(j + 1) * POOL_GC], wpool_ref[j]) for j in range(len(POOL_WINDOWS))],
        axis=1) * ps_ref[...]
    branches = (oa_ref[...], yb.astype(BF16), oc_ref[...])
    acc = None
    for n in range(N_BRANCH):
        gate = _sigmoid(_dot(hb, wgate_ref[:, n * D_MODEL:(n + 1) * D_MODEL]))
        term = gate * _dot(branches[n], wbr_ref[n])
        acc = term if acc is None else acc + term
    xn = x + mod_ref[0, 2:3, :] * _dot(acc.astype(BF16), wout_ref[...])
    xo_ref[...] = xn
    h2 = _modnorm(xn, g2_ref[...], mod_ref[0, 3:4, :], mod_ref[0, 4:5, :])
    h2_hi = h2.astype(BF16)
    if router:
        for j in range(D_MODEL // LANE):
            h2_ref[:, j, :] = h2[:, j * LANE:(j + 1) * LANE]
        h2_lo = (h2 - h2_hi.astype(F32)).astype(BF16)
        w_hi, w_lo = wr_hi_ref[...], wr_lo_ref[...]
        lg_ref[...] = (_dot(h2_hi, w_hi) + (_dot(h2_hi, w_lo) + _dot(h2_lo, w_hi))) + _dot(h2_lo, w_lo)
    else:
        h2_ref[...] = h2_hi


def _merge(x, oa, pooled, oc, g1, g2, mods, wgate, wpool, pscale, wbr, wout, tm, rows_per_mod, wr=None):
    m = x.shape[0]
    router = wr is not None
    mod_map = (lambda i: (i // (rows_per_mod // tm), 0, 0)) if rows_per_mod else (lambda i: (0, 0, 0))
    row = lambda w: pl.BlockSpec((tm, w), lambda i: (i, 0))
    full = lambda a: pl.BlockSpec(a.shape, lambda i: (0,) * a.ndim)
    in_specs = [row(D_MODEL), row(BRANCH_W), row(BRANCH_W), row(BRANCH_W), full(g1), full(g2),
                pl.BlockSpec((1, 6, D_MODEL), mod_map), full(wgate), full(wpool), full(pscale),
                full(wbr), full(wout)]
    args = [x, oa, pooled, oc, g1, g2, mods, wgate, wpool, pscale, wbr, wout]
    out_shape = [jax.ShapeDtypeStruct((m, D_MODEL), F32)]
    out_specs = [row(D_MODEL)]
    if router:
        in_specs += [full(wr[0]), full(wr[1])]
        args += [wr[0], wr[1]]
        out_shape += [jax.ShapeDtypeStruct((m, D_MODEL // LANE, LANE), F32), jax.ShapeDtypeStruct((m, LANE), F32)]
        out_specs += [pl.BlockSpec((tm, D_MODEL // LANE, LANE), lambda i: (i, 0, 0)), row(LANE)]
    else:
        out_shape.append(jax.ShapeDtypeStruct((m, D_MODEL), BF16))
        out_specs.append(row(D_MODEL))
    return pl.pallas_call(
        functools.partial(_merge_kernel, router=router),
        out_shape=out_shape,
        grid=(m // tm,),
        in_specs=in_specs,
        out_specs=out_specs,
        compiler_params=_cparams(("arbitrary",)),
        name="merge",
    )(*args)


def _final_norm(xn, fg):
    ms = jnp.mean(xn * xn, axis=-1, keepdims=True)
    return xn * lax.rsqrt(ms + EPS) * fg


def _swiglu(h, wg, wu, wd):
    a = _dot(h, wg)
    u = _dot(h, wu)
    return _dot(((a * _sigmoid(a)) * u).astype(BF16), wd)


def _ffn_kernel(h_ref, x_ref, mod_ref, wg_ref, wu_ref, wd_ref, o_ref, acc_ref):
    e = pl.program_id(1)

    @pl.when(e == 0)
    def _():
        acc_ref[...] = jnp.zeros_like(acc_ref)

    acc_ref[...] += _swiglu(h_ref[...], wg_ref[...], wu_ref[...], wd_ref[...])

    @pl.when(e == pl.num_programs(1) - 1)
    def _():
        o_ref[...] = x_ref[...] + mod_ref[0, 5:6, :] * acc_ref[...]


def _ffn(h2, x, mods, wg, wu, wd, tm, rows_per_mod):
    m = x.shape[0]
    mod_map = (lambda i, e: (i // (rows_per_mod // tm), 0, 0)) if rows_per_mod else (lambda i, e: (0, 0, 0))
    row = lambda w: pl.BlockSpec((tm, w), lambda i, e: (i, 0))
    return pl.pallas_call(
        _ffn_kernel,
        out_shape=jax.ShapeDtypeStruct((m, D_MODEL), F32),
        grid=(m // tm, wg.shape[1] // D_FF_E),
        in_specs=[row(D_MODEL), row(D_MODEL), pl.BlockSpec((1, 6, D_MODEL), mod_map),
                  pl.BlockSpec((D_MODEL, D_FF_E), lambda i, e: (0, e)),
                  pl.BlockSpec((D_MODEL, D_FF_E), lambda i, e: (0, e)),
                  pl.BlockSpec((D_FF_E, D_MODEL), lambda i, e: (e, 0))],
        out_specs=row(D_MODEL),
        scratch_shapes=[pltpu.VMEM((tm, D_MODEL), F32)],
        compiler_params=_cparams(("arbitrary", "arbitrary")),
        name="ffn",
    )(h2, x, mods, wg, wu, wd)


ROUTE_TILE = 512
EXPERT_TILE = 512
DISPATCH_CHUNK = 256
COMBINE_TILE = 256
N_SLAB = D_MODEL // LANE


def _route_kernel(lg_ref, slab_ref, cnt_ref, carry_ref):
    @pl.when(pl.program_id(0) == 0)
    def _():
        carry_ref[...] = jnp.zeros_like(carry_ref)

    logits = lg_ref[...]
    tr = logits.shape[0]
    lane = lax.broadcasted_iota(jnp.int32, logits.shape, 1).astype(F32)
    neg = -jnp.inf
    l1 = jnp.where(lane < N_EXP, logits, neg)
    m1 = jnp.max(l1, axis=-1, keepdims=True)
    i1 = jnp.min(jnp.where(l1 == m1, lane, float(LANE)), axis=-1, keepdims=True)
    l2 = jnp.where(lane == i1, neg, l1)
    m2 = jnp.max(l2, axis=-1, keepdims=True)
    i2 = jnp.min(jnp.where(l2 == m2, lane, float(LANE)), axis=-1, keepdims=True)
    e2 = jnp.exp(m2 - m1)
    den = 1.0 + e2
    oh1 = jnp.where(lane == i1, 1.0, 0.0)
    oh2 = jnp.where(lane == i2, 1.0, 0.0)
    oh = oh1 + oh2
    r = lax.broadcasted_iota(jnp.int32, (tr, tr), 0)
    c = lax.broadcasted_iota(jnp.int32, (tr, tr), 1)
    before = jnp.where(c < r, 1.0, 0.0).astype(BF16)
    seen = _dot(before, oh.astype(BF16)) + carry_ref[...]
    r1 = jnp.sum(seen * oh1, axis=-1, keepdims=True)
    r2 = jnp.sum(seen * oh2, axis=-1, keepdims=True)
    carry_ref[...] += jnp.sum(oh, axis=0, keepdims=True)
    slab = jnp.zeros_like(logits)
    for j, v in enumerate((i1, i2, r1, r2, 1.0 / den, e2 / den)):
        slab = jnp.where(lane == float(j), v, slab)
    slab_ref[...] = slab
    cnt_ref[...] = jnp.broadcast_to(carry_ref[...], cnt_ref.shape)


def _route(logits):
    m = logits.shape[0]
    return pl.pallas_call(
        _route_kernel,
        out_shape=[jax.ShapeDtypeStruct((m, LANE), F32), jax.ShapeDtypeStruct((8, LANE), F32)],
        grid=(m // ROUTE_TILE,),
        in_specs=[pl.BlockSpec((ROUTE_TILE, LANE), lambda i: (i, 0))],
        out_specs=[pl.BlockSpec((ROUTE_TILE, LANE), lambda i: (i, 0)), pl.BlockSpec((8, LANE), lambda i: (0, 0))],
        scratch_shapes=[pltpu.VMEM((1, LANE), F32)],
        compiler_params=_cparams(("arbitrary",)),
        name="route",
    )(logits)


def _row_copy(src, dst, src_row, dst_row, sem):
    return pltpu.make_async_copy(src.at[src_row], dst.at[dst_row], sem)


def _rows_wait(src, dst, n, sem):
    pltpu.make_async_copy(src.at[pl.ds(0, n)], dst.at[pl.ds(0, n)], sem).wait()


def _dispatch_kernel(d0_ref, d1_ref, zs_ref, h_hbm, xs_hbm, zbuf, sem, zsem, *, m):
    zbuf[...] = jnp.zeros_like(zbuf)
    for j in range(2 * N_EXP):
        cp = pltpu.make_async_copy(zbuf, xs_hbm.at[pl.ds(zs_ref[j], EXPERT_TILE)], zsem)
        cp.start()
        cp.wait()

    def issue(chunk):
        def row(k, carry):
            tok = chunk * DISPATCH_CHUNK + k
            _row_copy(h_hbm, xs_hbm, tok, d0_ref[tok], sem).start()
            _row_copy(h_hbm, xs_hbm, tok, d1_ref[tok], sem).start()
            return carry
        lax.fori_loop(0, DISPATCH_CHUNK, row, 0)

    issue(0)

    def step(chunk, carry):
        issue(chunk)
        _rows_wait(h_hbm, xs_hbm, TOP_K * DISPATCH_CHUNK, sem)
        return carry

    lax.fori_loop(1, m // DISPATCH_CHUNK, step, 0)
    _rows_wait(h_hbm, xs_hbm, TOP_K * DISPATCH_CHUNK, sem)


def _dispatch(dest0, dest1, zero_starts, h_slab, n_rows):
    m = h_slab.shape[0]
    return pl.pallas_call(
        functools.partial(_dispatch_kernel, m=m),
        out_shape=jax.ShapeDtypeStruct((n_rows, N_SLAB, LANE), F32),
        grid_spec=pltpu.PrefetchScalarGridSpec(
            num_scalar_prefetch=3,
            grid=(1,),
            in_specs=[pl.BlockSpec(memory_space=pl.ANY)],
            out_specs=pl.BlockSpec(memory_space=pl.ANY),
            scratch_shapes=[pltpu.VMEM((EXPERT_TILE, N_SLAB, LANE), F32),
                            pltpu.SemaphoreType.DMA, pltpu.SemaphoreType.DMA],
        ),
        compiler_params=_cparams(("arbitrary",)),
        name="dispatch",
    )(dest0, dest1, zero_starts, h_slab)


def _experts_kernel(te_ref, x_ref, wg_ref, wu_ref, wd_ref, y_ref):
    x = jnp.concatenate([x_ref[:, j, :] for j in range(N_SLAB)], axis=1).astype(BF16)
    y = _swiglu(x, wg_ref[...], wu_ref[...], wd_ref[...])
    for j in range(N_SLAB):
        y_ref[:, j, :] = y[:, j * LANE:(j + 1) * LANE]


def _experts(tile_expert, xs, wg, wu, wd):
    n_rows = xs.shape[0]
    slab = pl.BlockSpec((EXPERT_TILE, N_SLAB, LANE), lambda i, te: (i, 0, 0))
    return pl.pallas_call(
        _experts_kernel,
        out_shape=jax.ShapeDtypeStruct((n_rows, N_SLAB, LANE), F32),
        grid_spec=pltpu.PrefetchScalarGridSpec(
            num_scalar_prefetch=1,
            grid=(n_rows // EXPERT_TILE,),
            in_specs=[slab,
                      pl.BlockSpec((None, D_MODEL, D_FF_E), lambda i, te: (te[i], 0, 0)),
                      pl.BlockSpec((None, D_MODEL, D_FF_E), lambda i, te: (te[i], 0, 0)),
                      pl.BlockSpec((None, D_FF_E, D_MODEL), lambda i, te: (te[i], 0, 0))],
            out_specs=slab,
        ),
        compiler_params=_cparams(("arbitrary",)),
        name="experts",
    )(tile_expert, xs, wg, wu, wd)


def _combine_kernel(*refs, final):
    if final:
        d0_ref, d1_ref, x_ref, mod_ref, slab_ref, fg_ref, ys_hbm, o_ref, buf, sem = refs
    else:
        d0_ref, d1_ref, x_ref, mod_ref, slab_ref, ys_hbm, o_ref, buf, sem = refs
    tc = COMBINE_TILE
    base = pl.program_id(0) * tc

    def issue(k, carry):
        _row_copy(ys_hbm, buf.at[0], d0_ref[base + k], k, sem).start()
        _row_copy(ys_hbm, buf.at[1], d1_ref[base + k], k, sem).start()
        return carry

    lax.fori_loop(0, tc, issue, 0)
    _rows_wait(ys_hbm, buf.at[0], tc, sem)
    _rows_wait(ys_hbm, buf.at[1], tc, sem)
    slab = slab_ref[...]
    w1, w2 = slab[:, 4:5], slab[:, 5:6]
    y1 = jnp.concatenate([buf[0, :, j, :] for j in range(N_SLAB)], axis=1)
    y2 = jnp.concatenate([buf[1, :, j, :] for j in range(N_SLAB)], axis=1)
    xn = x_ref[...] + mod_ref[0, 5:6, :] * (w1 * y1 + w2 * y2)
    o_ref[...] = _final_norm(xn, fg_ref[...]) if final else xn


def _combine(dest0, dest1, x, mods, slab, ys, rows_per_mod, final_g=None):
    m = x.shape[0]
    tc = COMBINE_TILE
    final = final_g is not None
    mod_map = ((lambda i, a, b: (i // (rows_per_mod // tc), 0, 0)) if rows_per_mod
               else (lambda i, a, b: (0, 0, 0)))
    in_specs = [pl.BlockSpec((tc, D_MODEL), lambda i, a, b: (i, 0)),
                pl.BlockSpec((1, 6, D_MODEL), mod_map),
                pl.BlockSpec((tc, LANE), lambda i, a, b: (i, 0))]
    args = [x, mods, slab]
    if final:
        in_specs.append(pl.BlockSpec((1, D_MODEL), lambda i, a, b: (0, 0)))
        args.append(final_g)
    in_specs.append(pl.BlockSpec(memory_space=pl.ANY))
    args.append(ys)
    return pl.pallas_call(
        functools.partial(_combine_kernel, final=final),
        out_shape=jax.ShapeDtypeStruct((m, D_MODEL), F32),
        grid_spec=pltpu.PrefetchScalarGridSpec(
            num_scalar_prefetch=2,
            grid=(m // tc,),
            in_specs=in_specs,
            out_specs=pl.BlockSpec((tc, D_MODEL), lambda i, a, b: (i, 0)),
            scratch_shapes=[pltpu.VMEM((2, tc, N_SLAB, LANE), F32), pltpu.SemaphoreType.DMA],
        ),
        compiler_params=_cparams(("arbitrary",)),
        name="combine",
    )(dest0, dest1, *args)


def _moe(h_slab, logits, x, mods, wg, wu, wd, rows_per_mod, final_g=None):
    m = x.shape[0]
    t = EXPERT_TILE
    n_rows = TOP_K * m + N_EXP * t
    slab, counts = _route(logits)
    cnt = counts[0, :N_EXP].astype(jnp.int32)
    padded = (cnt + (t - 1)) // t * t
    ends = jnp.cumsum(padded)
    starts = ends - padded
    dest = jnp.take(starts, slab[:, 0:2].astype(jnp.int32)) + slab[:, 2:4].astype(jnp.int32)
    dest0, dest1 = dest[:, 0], dest[:, 1]
    tile_start = jnp.arange(n_rows // t, dtype=jnp.int32) * t
    tile_expert = jnp.minimum(jnp.sum(tile_start[:, None] >= ends[None, :], axis=1), N_EXP - 1).astype(jnp.int32)
    zero_starts = jnp.concatenate([
        jnp.maximum(ends - t, 0),
        jnp.minimum(ends[-1] + jnp.arange(N_EXP, dtype=jnp.int32) * t, n_rows - t)]).astype(jnp.int32)
    xs = _dispatch(dest0, dest1, zero_starts, h_slab, n_rows)
    ys = _experts(tile_expert, xs, wg, wu, wd)
    return _combine(dest0, dest1, x, mods, slab, ys, rows_per_mod, final_g)


def _prep_w_in(w):
    qa, ka, va, ub, qc = (w[:, 0:512], w[:, 512:1024], w[:, 1024:1536], w[:, 1536:2048], w[:, 2048:2560])
    kc, vc = w[:, 2560:2688], w[:, 2688:2816]
    dup = lambda a: jnp.concatenate([a[:, 0:64], a[:, 0:64], a[:, 64:128], a[:, 64:128]], axis=1)
    scale = LOG2E * DHA ** -0.5
    return jnp.concatenate([qa * scale, ka, va, qc * scale, dup(kc), dup(vc), ub], axis=1).astype(BF16)


def _rope_tables(t):
    rows = t // GRID_W
    row = jnp.repeat(jnp.arange(rows, dtype=F32), GRID_W)
    col = jnp.tile(jnp.arange(GRID_W, dtype=F32), rows)
    inv = ROPE_BASE ** (-jnp.arange(0, ROT_HALF, 2, dtype=F32) / ROT_HALF)
    ang_r, ang_c = row[:, None] * inv, col[:, None] * inv
    zero = jnp.zeros_like(ang_r)
    seg = lambda a, b: jnp.concatenate([a, b], axis=1)
    cos = jnp.concatenate([seg(jnp.cos(ang_r), jnp.cos(ang_r)), seg(jnp.cos(ang_c), jnp.cos(ang_c))], axis=1)
    s_next = jnp.concatenate([seg(-jnp.sin(ang_r), zero), seg(-jnp.sin(ang_c), zero)], axis=1)
    s_prev = jnp.concatenate([seg(zero, jnp.sin(ang_r)), seg(zero, jnp.sin(ang_c))], axis=1)
    two = lambda a: jnp.concatenate([a, a], axis=1)
    return two(cos), two(s_next), two(s_prev)


def kernel(x_prompt, x_sample, cache_diff_k, cache_diff_v, cache_win_k, cache_win_v, c, c_ctx, w_ada, b_ada, norm1_g, norm2_g, w_in, w_gate, lam_p, subln_g, w_pool, pool_scale, sink, w_branch, w_out, w_ff_gate, w_ff_up, w_ff_down, w_router, w_ex_gate, w_ex_up, w_ex_down, final_g):
    bp, s, _ = x_prompt.shape
    bs, t, _ = x_sample.shape
    past = cache_diff_k.shape[2]

    cvec = jnp.concatenate([c_ctx[None, :], c, jnp.zeros((MOD_ROWS - 1 - bs, D_MODEL), F32)], axis=0)
    mods = _adaln(cvec, w_ada, b_ada).reshape(DEPTH, MOD_ROWS, 6, D_MODEL)
    rope_tabs = _rope_tables(t)

    ck_a = cache_diff_k.reshape(bs, DEPTH, past, HA * 2 * DHA)
    cv_a = cache_diff_v.reshape(bs, DEPTH, past, HA * 2 * DHA)
    ck_c = cache_win_k.reshape(bs, DEPTH, past, KVC * DHC)
    cv_c = cache_win_v.reshape(bs, DEPTH, past, KVC * DHC)

    xc = x_prompt.reshape(bp * s, D_MODEL)
    xl = x_sample.reshape(bs * t, D_MODEL)
    caches = [[], [], [], []]
    for l in range(DEPTH):
        last = l == DEPTH - 1
        lam_init = 0.8 - 0.6 * math.exp(-0.3 * l)
        w_in_l = _prep_w_in(w_in[l])
        wgate_l = w_gate[l].astype(BF16)
        wpool_l = w_pool[l].astype(BF16)
        wbr_l = w_branch[l].astype(BF16)
        wout_l = w_out[l].astype(BF16)
        g1 = norm1_g[l][None, :]
        g2 = norm2_g[l][None, :]
        sg = subln_g[l][None, :]
        ps = pool_scale[l][None, :]
        fg = final_g[None, :] if last else None
        i = l // 2
        if l % 2 == 0:
            wr = None
            wg, wu, wd = (w_ff_gate[i].astype(BF16), w_ff_up[i].astype(BF16), w_ff_down[i].astype(BF16))
        else:
            wr_f = jnp.pad(w_router[i], ((0, 0), (0, LANE - N_EXP)))
            wr_hi = wr_f.astype(BF16)
            wr = (wr_hi, (wr_f - wr_hi.astype(F32)).astype(BF16))
            wg, wu, wd = (w_ex_gate[i].astype(BF16), w_ex_up[i].astype(BF16), w_ex_down[i].astype(BF16))

        def channel_mixer(res, mod, rows_per_mod):
            if wr is None:
                assert fg is None
                return _ffn(res[1], res[0], mod, wg, wu, wd, tm=512, rows_per_mod=rows_per_mod)
            return _moe(res[1], res[2], res[0], mod, wg, wu, wd, rows_per_mod, final_g=fg)

        m_c = mods[l, 0:1]
        pc, ubc, dk, dv, wk, wv = _inproj(xc, g1, m_c, w_in_l, tm=256, rows_per_mod=0)
        for lst, a in zip(caches, (dk, dv, wk, wv)):
            lst.append(a)
        oa = _diff_ctx(pc, lam_p[l], sg, lam_init, bp, s)
        pooled = _pool(ubc, bp, s)
        oc = _gqa_ctx(pc, sink[l], bp, s)
        res = _merge(xc, oa, pooled, oc, g1, g2, m_c, wgate_l, wpool_l, ps, wbr_l, wout_l,
                     tm=256, rows_per_mod=0, wr=wr)
        xc = channel_mixer(res, m_c, 0)

        m_l = mods[l, 1:1 + bs]
        pl_, ubl = _inproj(xl, g1, m_l, w_in_l, tm=256, rows_per_mod=t, rope_tabs=rope_tabs)
        oa = _diff_lat(pl_, ck_a, cv_a, l, lam_p[l], sg, lam_init, bs, t, tq=128)
        pooled = _pool(ubl, bs, t)
        oc = _gqa_lat(pl_, ck_c, cv_c, l, sink[l], bs, t)
        res = _merge(xl, oa, pooled, oc, g1, g2, m_l, wgate_l, wpool_l, ps, wbr_l, wout_l,
                     tm=256, rows_per_mod=t, wr=wr)
        xl = channel_mixer(res, m_l, t)

    y_prompt = xc.reshape(bp, s, D_MODEL)
    y_sample = xl.reshape(bs, t, D_MODEL)
    new_diff_k = jnp.stack(caches[0], axis=0).reshape(DEPTH, bp, s, HA, 2, DHA).swapaxes(0, 1)
    new_diff_v = jnp.stack(caches[1], axis=0).reshape(DEPTH, bp, s, HA, 2 * DHA).swapaxes(0, 1)
    new_win_k = jnp.stack(caches[2], axis=0).reshape(DEPTH, bp, s, KVC, DHC).swapaxes(0, 1)
    new_win_v = jnp.stack(caches[3], axis=0).reshape(DEPTH, bp, s, KVC, DHC).swapaxes(0, 1)
    return (y_prompt, y_sample, new_diff_k, new_diff_v, new_win_k, new_win_v)
```

```python
import functools
import math

import numpy as np
import jax
import jax.numpy as jnp
from jax import lax
from jax.experimental import pallas as pl
from jax.experimental.pallas import tpu as pltpu

F32 = jnp.float32
BF16 = jnp.bfloat16

D_MODEL = 1024
DEPTH = 2
GRID_W = 64
HA = 4
DHA = 64
HC = 8
KVC = 2
REPC = HC // KVC
DHC = 64
WINDOW = 128
QBLK = 128
POOL_WINDOWS = (2, 4, 8, 16)
POOL_GC = 128
POOL_W = 512
BRANCH_W = 512
N_BRANCH = 3
ROPE_BASE = 10000.0
ROT_HALF = 32
N_EXP = 8
TOP_K = 2
D_FF_E = 1408
EPS = 1e-6
NEG_INF = -1e30
LOG2E = math.log2(math.e)

LANE = 128
MOD_ROWS = 16
P_W = 20 * LANE
IN_W2 = 24 * LANE
VMEM_LIMIT = 56 * 1024 * 1024


def _cparams(sem):
    return pltpu.CompilerParams(dimension_semantics=sem, vmem_limit_bytes=VMEM_LIMIT)


def _dot(a, b):
    return jnp.dot(a, b, preferred_element_type=F32)


def _dot_nt(a, b):
    return lax.dot_general(a, b, (((1,), (1,)), ((), ())), preferred_element_type=F32)


def _sigmoid(x):
    return 1.0 / (1.0 + jnp.exp(-x))


def _modnorm(x, g, shift, scale):
    ms = jnp.mean(x * x, axis=-1, keepdims=True)
    return (x * lax.rsqrt(ms + EPS) * g) * (1.0 + scale) + shift


def _adaln_kernel(c_ref, w_ref, b_ref, o_ref):
    c = c_ref[...]
    s = c * _sigmoid(c)
    o_ref[0] = _dot(s.astype(BF16), w_ref[0].astype(BF16)) + b_ref[0]


def _adaln(cvec, w_ada, b_ada):
    tn = 1536
    n = 6 * D_MODEL
    return pl.pallas_call(
        _adaln_kernel,
        out_shape=jax.ShapeDtypeStruct((DEPTH, MOD_ROWS, n), F32),
        grid=(DEPTH, n // tn),
        in_specs=[
            pl.BlockSpec((MOD_ROWS, D_MODEL), lambda l, j: (0, 0)),
            pl.BlockSpec((1, D_MODEL, tn), lambda l, j: (l, 0, j)),
            pl.BlockSpec((1, 1, tn), lambda l, j: (l, 0, j)),
        ],
        out_specs=pl.BlockSpec((1, MOD_ROWS, tn), lambda l, j: (l, 0, j)),
        compiler_params=_cparams(("arbitrary", "arbitrary")),
        name="adaln",
    )(cvec, w_ada, b_ada.reshape(DEPTH, 1, n))


def _rope(p, cos, sin_next, sin_prev):
    nxt = pltpu.roll(p, LANE - 16, 1)
    prv = pltpu.roll(p, 16, 1)
    return p * cos + nxt * sin_next + prv * sin_prev


def _inproj_kernel(*refs, rope):
    if rope:
        (x_ref, g_ref, mod_ref, w_ref, cos_ref, sn_ref, sp_ref, p_ref, ub_ref) = refs
    else:
        (x_ref, g_ref, mod_ref, w_ref, p_ref, ub_ref, dk_ref, dv_ref, wk_ref, wv_ref) = refs
    h = _modnorm(x_ref[...], g_ref[...], mod_ref[0, 0:1, :], mod_ref[0, 1:2, :])
    hb = h.astype(BF16)
    if rope:
        cos, sn, sp = cos_ref[...], sn_ref[...], sp_ref[...]
    lane = lax.broadcasted_iota(jnp.int32, (1, LANE), 1)
    for c in range(IN_W2 // 512):
        pc = _dot(hb, w_ref[:, c * 512:(c + 1) * 512])
        blocks = [pc[:, j * LANE:(j + 1) * LANE] for j in range(4)]
        if c == 5:
            ub_ref[...] = pc
            continue
        if rope:
            n_rot = {0: 4, 1: 4, 2: 0, 3: 4, 4: 2}[c]
            blocks = [_rope(b, cos, sn, sp) if j < n_rot else b for j, b in enumerate(blocks)]
        elif c == 1:
            dk_ref[...] = pc
        elif c == 2:
            dv_ref[...] = pc
        elif c == 4:
            wk_ref[...] = jnp.where(lane < DHC, blocks[0], blocks[1])
            wv_ref[...] = jnp.where(lane < DHC, blocks[2], blocks[3])
        for j in range(4):
            p_ref[:, (4 * c + j) * LANE:(4 * c + j + 1) * LANE] = blocks[j].astype(BF16)


def _inproj(x, g, mods, w, tm, rows_per_mod, rope_tabs=None):
    m = x.shape[0]
    rope = rope_tabs is not None
    mod_map = (lambda i: (i // (rows_per_mod // tm), 0, 0)) if rows_per_mod else (lambda i: (0, 0, 0))
    in_specs = [
        pl.BlockSpec((tm, D_MODEL), lambda i: (i, 0)),
        pl.BlockSpec((1, D_MODEL), lambda i: (0, 0)),
        pl.BlockSpec((1, 6, D_MODEL), mod_map),
        pl.BlockSpec((D_MODEL, IN_W2), lambda i: (0, 0)),
    ]
    args = [x, g, mods, w]
    out_shape = [jax.ShapeDtypeStruct((m, P_W), BF16), jax.ShapeDtypeStruct((m, POOL_W), F32)]
    out_specs = [pl.BlockSpec((tm, P_W), lambda i: (i, 0)), pl.BlockSpec((tm, POOL_W), lambda i: (i, 0))]
    if rope:
        t = rope_tabs[0].shape[0]
        for tab in rope_tabs:
            in_specs.append(pl.BlockSpec((tm, LANE), lambda i: (i % (t // tm), 0)))
            args.append(tab)
    else:
        for wdt in (512, 512, LANE, LANE):
            out_shape.append(jax.ShapeDtypeStruct((m, wdt), F32))
            out_specs.append(pl.BlockSpec((tm, wdt), lambda i: (i, 0)))
    return pl.pallas_call(
        functools.partial(_inproj_kernel, rope=rope),
        out_shape=out_shape,
        grid=(m // tm,),
        in_specs=in_specs,
        out_specs=out_specs,
        compiler_params=_cparams(("arbitrary",)),
        name="inproj_rope" if rope else "inproj",
    )(*args)


def _diff_lambda(lam_ref, lam_init):
    lf = lam_ref[...]
    s1 = jnp.sum(lf[0:1] * lf[1:2], axis=-1, keepdims=True)
    s2 = jnp.sum(lf[2:3] * lf[3:4], axis=-1, keepdims=True)
    return jnp.exp(s1) - jnp.exp(s2) + lam_init


def _diff_scores(q, k):
    lane = lax.broadcasted_iota(jnp.int32, (1, LANE), 1)
    zero = jnp.zeros_like(q)
    q2 = jnp.concatenate([jnp.where(lane < DHA, q, zero), jnp.where(lane < DHA, zero, q)], axis=0)
    return _dot_nt(q2, k)


def _diff_finish(s, v, lam, g, lam_init):
    tq = s.shape[0] // 2
    e = jnp.exp2(s - jnp.max(s, axis=-1, keepdims=True))
    inv = 1.0 / jnp.sum(e, axis=-1, keepdims=True)
    o2 = _dot(e.astype(BF16), v) * inv
    o = o2[:tq] - lam * o2[tq:]
    ms = jnp.mean(o * o, axis=-1, keepdims=True)
    return ((o * lax.rsqrt(ms + EPS) * g) * (1.0 - lam_init)).astype(BF16)


def _diff_tile(q, k, v, lam, g, lam_init):
    return _diff_finish(_diff_scores(q, k), v, lam, g, lam_init)


def _diff_ctx_kernel(q_ref, k_ref, v_ref, lam_ref, g_ref, o_ref, *, lam_init):
    lam = _diff_lambda(lam_ref, lam_init)
    g = g_ref[...]
    for h in range(HA):
        sl = slice(h * LANE, (h + 1) * LANE)
        o_ref[:, sl] = _diff_tile(q_ref[:, sl], k_ref[:, sl], v_ref[:, sl], lam, g, lam_init)


def _diff_ctx(p, lam_p, subln_g, lam_init, nb, s):
    w = HA * LANE
    return pl.pallas_call(
        functools.partial(_diff_ctx_kernel, lam_init=lam_init),
        out_shape=jax.ShapeDtypeStruct((nb * s, w), BF16),
        grid=(nb,),
        in_specs=[
            pl.BlockSpec((s, w), lambda b: (b, 0)),
            pl.BlockSpec((s, w), lambda b: (b, 1)),
            pl.BlockSpec((s, w), lambda b: (b, 2)),
            pl.BlockSpec((4, DHA), lambda b: (0, 0)),
            pl.BlockSpec((1, LANE), lambda b: (0, 0)),
        ],
        out_specs=pl.BlockSpec((s, w), lambda b: (b, 0)),
        compiler_params=_cparams(("arbitrary",)),
        name="diff_ctx",
    )(p, p, p, lam_p, subln_g)


def _diff_lat_kernel(q_ref, kn_ref, vn_ref, kc_ref, vc_ref, lam_ref, g_ref, o_ref,
                     k_all, v_all, s_even, s_odd, *, lam_init, past, tq):
    k_all[0:past, :] = kc_ref[0, 0].astype(BF16)
    v_all[0:past, :] = vc_ref[0, 0].astype(BF16)
    k_all[past:, :] = kn_ref[...]
    v_all[past:, :] = vn_ref[...]
    lam = _diff_lambda(lam_ref, lam_init)
    g = g_ref[...]
    n_tiles = q_ref.shape[0] // tq

    def rows(i):
        return pl.ds(pl.multiple_of(i * tq, tq), tq)

    s_even[...] = _diff_scores(q_ref[rows(0), :], k_all[...])

    def body(j, carry):
        i = 2 * j
        s_odd[...] = _diff_scores(q_ref[rows(i + 1), :], k_all[...])
        o_ref[rows(i), :] = _diff_finish(s_even[...], v_all[...], lam, g, lam_init)
        s_even[...] = _diff_scores(q_ref[rows(jnp.minimum(i + 2, n_tiles - 1)), :], k_all[...])
        o_ref[rows(i + 1), :] = _diff_finish(s_odd[...], v_all[...], lam, g, lam_init)
        return carry

    lax.fori_loop(0, n_tiles // 2, body, 0)


def _diff_lat(p, cache_k, cache_v, l, lam_p, subln_g, lam_init, nb, t, tq):
    past = cache_k.shape[2]
    return pl.pallas_call(
        functools.partial(_diff_lat_kernel, lam_init=lam_init, past=past, tq=tq),
        out_shape=jax.ShapeDtypeStruct((nb * t, HA * LANE), BF16),
        grid=(nb, HA),
        in_specs=[
            pl.BlockSpec((t, LANE), lambda b, h: (b, h)),
            pl.BlockSpec((t, LANE), lambda b, h: (b, 4 + h)),
            pl.BlockSpec((t, LANE), lambda b, h: (b, 8 + h)),
            pl.BlockSpec((1, 1, past, LANE), lambda b, h: (b, l, 0, h)),
            pl.BlockSpec((1, 1, past, LANE), lambda b, h: (b, l, 0, h)),
            pl.BlockSpec((4, DHA), lambda b, h: (0, 0)),
            pl.BlockSpec((1, LANE), lambda b, h: (0, 0)),
        ],
        out_specs=pl.BlockSpec((t, LANE), lambda b, h: (b, h)),
        scratch_shapes=[pltpu.VMEM((past + t, LANE), BF16), pltpu.VMEM((past + t, LANE), BF16),
                        pltpu.VMEM((2 * tq, past + t), F32), pltpu.VMEM((2 * tq, past + t), F32)],
        compiler_params=_cparams(("arbitrary", "arbitrary")),
        name="diff_lat",
    )(p, p, p, cache_k, cache_v, lam_p, subln_g)


POOL_PAD = 16


def _pool_kernel(u_ref, o_ref, pad_ref, *, t):
    tp = t + POOL_PAD
    pos = lax.broadcasted_iota(jnp.int32, (t, 1), 0)
    for gi, w in enumerate(POOL_WINDOWS):
        sl = slice(gi * POOL_GC, (gi + 1) * POOL_GC)
        u = u_ref[0, :, sl]
        pad_ref[0:t, :] = u
        pad_ref[t:tp, :] = jnp.zeros((POOL_PAD, POOL_GC), F32)
        acc = pad_ref[...]
        step = 1
        while step < w:
            acc = acc + pltpu.roll(acc, step, 0)
            step *= 2
        ahead = w - w // 2 - 1
        if ahead:
            acc = pltpu.roll(acc, tp - ahead, 0)
        lo = jnp.maximum(pos - w // 2, 0)
        hi = jnp.minimum(pos + (w - w // 2), t)
        mean = acc[0:t] / (hi - lo).astype(F32)
        o_ref[0, :, sl] = (mean - u).astype(BF16)


def _pool(ub, nb, t):
    return pl.pallas_call(
        functools.partial(_pool_kernel, t=t),
        out_shape=jax.ShapeDtypeStruct((nb, t, POOL_W), BF16),
        grid=(nb,),
        in_specs=[pl.BlockSpec((1, t, POOL_W), lambda b: (b, 0, 0))],
        out_specs=pl.BlockSpec((1, t, POOL_W), lambda b: (b, 0, 0)),
        scratch_shapes=[pltpu.VMEM((t + POOL_PAD, POOL_GC), F32)],
        compiler_params=_cparams(("arbitrary",)),
        name="pool",
    )(ub.reshape(nb, t, POOL_W)).reshape(nb * t, POOL_W)


def _stack_heads(q):
    lane = lax.broadcasted_iota(jnp.int32, (1, LANE), 1)
    zero = jnp.zeros_like(q[:, :LANE])
    parts = []
    for pair in range(2):
        qp = q[:, pair * LANE:(pair + 1) * LANE]
        parts.append(jnp.where(lane < DHC, qp, zero))
        parts.append(jnp.where(lane < DHC, zero, qp))
    return jnp.concatenate(parts, axis=0)


def _gqa_core(score_parts, masks, value_parts, sink_ref, g, tq):
    lane = lax.broadcasted_iota(jnp.int32, (1, LANE), 1)
    e_parts = [[] for _ in score_parts]
    invs = []
    for r in range(REPC):
        rows = slice(r * tq, (r + 1) * tq)
        sink = sink_ref[g * REPC + r] * LOG2E
        parts = [s[rows] if mk is None else jnp.where(mk, s[rows], NEG_INF)
                 for s, mk in zip(score_parts, masks)]
        m = jnp.maximum(functools.reduce(jnp.maximum, [jnp.max(s, axis=-1, keepdims=True) for s in parts]), sink)
        es = [jnp.exp2(s - m) for s in parts]
        den = functools.reduce(lambda a, b: a + b, [jnp.sum(e, axis=-1, keepdims=True) for e in es])
        invs.append(1.0 / (den + jnp.exp2(sink - m)))
        for lst, e in zip(e_parts, es):
            lst.append(e.astype(BF16))
    o = functools.reduce(lambda a, b: a + b,
                         [_dot(jnp.concatenate(lst, axis=0), v) for lst, v in zip(e_parts, value_parts)])
    o = o * jnp.concatenate(invs, axis=0)
    outs = [o[r * tq:(r + 1) * tq] for r in range(REPC)]
    return jnp.concatenate([jnp.where(lane < DHC, outs[0], outs[1]),
                            jnp.where(lane < DHC, outs[2], outs[3])], axis=1).astype(BF16)


def _gqa_ctx_kernel(sink_ref, q_ref, k_ref, v_ref, o_ref, *, s):
    for g in range(KVC):
        q4 = _stack_heads(q_ref[:, g * 2 * LANE:(g + 1) * 2 * LANE])
        sc = _dot_nt(q4, k_ref[:, g * LANE:(g + 1) * LANE])
        o_ref[:, g * 2 * LANE:(g + 1) * 2 * LANE] = _gqa_core(
            [sc], [None], [v_ref[:, g * LANE:(g + 1) * LANE]], sink_ref, g, s)


def _gqa_ctx(p, sink, nb, s):
    w = HC * DHC
    return pl.pallas_call(
        functools.partial(_gqa_ctx_kernel, s=s),
        out_shape=jax.ShapeDtypeStruct((nb * s, w), BF16),
        grid=(nb,),
        in_specs=[
            pl.BlockSpec(memory_space=pltpu.SMEM),
            pl.BlockSpec((s, w), lambda b: (b, 3)),
            pl.BlockSpec((s, KVC * LANE), lambda b: (b, 8)),
            pl.BlockSpec((s, KVC * LANE), lambda b: (b, 9)),
        ],
        out_specs=pl.BlockSpec((s, w), lambda b: (b, 0)),
        compiler_params=_cparams(("arbitrary",)),
        name="gqa_ctx",
    )(sink, p, p, p)


def _gqa_lat_kernel(sink_ref, q_ref, kn_ref, vn_ref, kc_ref, vc_ref, o_ref, kc_dup, vc_dup,
                    sc_even, sw_even, sc_odd, sw_odd, *, t):
    g = pl.program_id(1)
    lane = lax.broadcasted_iota(jnp.int32, (1, LANE), 1)
    keep = jnp.where(lane < DHC, 0, 1) == g
    for src, dst in ((kc_ref, kc_dup), (vc_ref, vc_dup)):
        blk = src[0, 0]
        dst[...] = jnp.where(keep, blk, pltpu.roll(blk, DHC, 1)).astype(BF16)
    span = QBLK + 2 * WINDOW
    n_blocks = t // QBLK

    def rows(i):
        return pl.ds(pl.multiple_of(i * QBLK, QBLK), QBLK)

    def win_start(i):
        return pl.multiple_of(jnp.clip((i - 1) * QBLK, 0, t - span), QBLK)

    def scores(i, sc_ref, sw_ref):
        q4 = _stack_heads(q_ref[rows(i), :])
        sc_ref[...] = _dot_nt(q4, kc_dup[...])
        sw_ref[...] = _dot_nt(q4, kn_ref[pl.ds(win_start(i), span), :])

    def finish(i, sc_ref, sw_ref):
        w0 = win_start(i)
        qpos = i * QBLK + lax.broadcasted_iota(jnp.int32, (QBLK, span), 0)
        kpos = w0 + lax.broadcasted_iota(jnp.int32, (QBLK, span), 1)
        band = jnp.abs(qpos - kpos) <= WINDOW
        o_ref[rows(i), :] = _gqa_core([sc_ref[...], sw_ref[...]], [None, band],
                                      [vc_dup[...], vn_ref[pl.ds(w0, span), :]], sink_ref, g, QBLK)

    scores(0, sc_even, sw_even)

    def body(j, carry):
        i = 2 * j
        scores(i + 1, sc_odd, sw_odd)
        finish(i, sc_even, sw_even)
        scores(jnp.minimum(i + 2, n_blocks - 1), sc_even, sw_even)
        finish(i + 1, sc_odd, sw_odd)
        return carry

    lax.fori_loop(0, n_blocks // 2, body, 0)


def _gqa_lat(p, cache_k, cache_v, l, sink, nb, t):
    past = cache_k.shape[2]
    return pl.pallas_call(
        functools.partial(_gqa_lat_kernel, t=t),
        out_shape=jax.ShapeDtypeStruct((nb * t, HC * DHC), BF16),
        grid=(nb, KVC),
        in_specs=[
            pl.BlockSpec(memory_space=pltpu.SMEM),
            pl.BlockSpec((t, 2 * LANE), lambda b, g: (b, 6 + g)),
            pl.BlockSpec((t, LANE), lambda b, g: (b, 16 + g)),
            pl.BlockSpec((t, LANE), lambda b, g: (b, 18 + g)),
            pl.BlockSpec((1, 1, past, LANE), lambda b, g: (b, l, 0, 0)),
            pl.BlockSpec((1, 1, past, LANE), lambda b, g: (b, l, 0, 0)),
        ],
        out_specs=pl.BlockSpec((t, 2 * LANE), lambda b, g: (b, g)),
        scratch_shapes=[pltpu.VMEM((past, LANE), BF16), pltpu.VMEM((past, LANE), BF16)]
        + [pltpu.VMEM((REPC * QBLK, past), F32), pltpu.VMEM((REPC * QBLK, QBLK + 2 * WINDOW), F32)] * 2,
        compiler_params=_cparams(("arbitrary", "arbitrary")),
        name="gqa_lat",
    )(sink, p, p, p, cache_k, cache_v)


def _merge_kernel(*refs, router):
    (x_ref, oa_ref, pb_ref, oc_ref, g1_ref, g2_ref, mod_ref, wgate_ref, wpool_ref, ps_ref,
     wbr_ref, wout_ref) = refs[:12]
    if router:
        wr_hi_ref, wr_lo_ref, xo_ref, h2_ref, lg_ref = refs[12:]
    else:
        xo_ref, h2_ref = refs[12:]
    x = x_ref[...]
    h = _modnorm(x, g1_ref[...], mod_ref[0, 0:1, :], mod_ref[0, 1:2, :])
    hb = h.astype(BF16)
    pooled = pb_ref[...]
    yb = jnp.concatenate(
        [_dot(pooled[:, j * POOL_GC:(j + 1) * POOL_GC], wpool_ref[j]) for j in range(len(POOL_WINDOWS))],
        axis=1) * ps_ref[...]
    branches = (oa_ref[...], yb.astype(BF16), oc_ref[...])
    acc = None
    for n in range(N_BRANCH):
        gate = _sigmoid(_dot(hb, wgate_ref[:, n * D_MODEL:(n + 1) * D_MODEL]))
        term = gate * _dot(branches[n], wbr_ref[n])
        acc = term if acc is None else acc + term
    xn = x + mod_ref[0, 2:3, :] * _dot(acc.astype(BF16), wout_ref[...])
    xo_ref[...] = xn
    h2 = _modnorm(xn, g2_ref[...], mod_ref[0, 3:4, :], mod_ref[0, 4:5, :])
    h2_hi = h2.astype(BF16)
    if router:
        for j in range(D_MODEL // LANE):
            h2_ref[:, j, :] = h2[:, j * LANE:(j + 1) * LANE]
        h2_lo = (h2 - h2_hi.astype(F32)).astype(BF16)
        w_hi, w_lo = wr_hi_ref[...], wr_lo_ref[...]
        lg_ref[...] = (_dot(h2_hi, w_hi) + (_dot(h2_hi, w_lo) + _dot(h2_lo, w_hi))) + _dot(h2_lo, w_lo)
    else:
        h2_ref[...] = h2_hi


def _merge(x, oa, pooled, oc, g1, g2, mods, wgate, wpool, pscale, wbr, wout, tm, rows_per_mod, wr=None):
    m = x.shape[0]
    router = wr is not None
    mod_map = (lambda i: (i // (rows_per_mod // tm), 0, 0)) if rows_per_mod else (lambda i: (0, 0, 0))
    row = lambda w: pl.BlockSpec((tm, w), lambda i: (i, 0))
    full = lambda a: pl.BlockSpec(a.shape, lambda i: (0,) * a.ndim)
    in_specs = [row(D_MODEL), row(BRANCH_W), row(BRANCH_W), row(BRANCH_W), full(g1), full(g2),
                pl.BlockSpec((1, 6, D_MODEL), mod_map), full(wgate), full(wpool), full(pscale),
                full(wbr), full(wout)]
    args = [x, oa, pooled, oc, g1, g2, mods, wgate, wpool, pscale, wbr, wout]
    out_shape = [jax.ShapeDtypeStruct((m, D_MODEL), F32)]
    out_specs = [row(D_MODEL)]
    if router:
        in_specs += [full(wr[0]), full(wr[1])]
        args += [wr[0], wr[1]]
        out_shape += [jax.ShapeDtypeStruct((m, D_MODEL // LANE, LANE), F32), jax.ShapeDtypeStruct((m, LANE), F32)]
        out_specs += [pl.BlockSpec((tm, D_MODEL // LANE, LANE), lambda i: (i, 0, 0)), row(LANE)]
    else:
        out_shape.append(jax.ShapeDtypeStruct((m, D_MODEL), BF16))
        out_specs.append(row(D_MODEL))
    return pl.pallas_call(
        functools.partial(_merge_kernel, router=router),
        out_shape=out_shape,
        grid=(m // tm,),
        in_specs=in_specs,
        out_specs=out_specs,
        compiler_params=_cparams(("arbitrary",)),
        name="merge",
    )(*args)


def _final_norm(xn, fg):
    ms = jnp.mean(xn * xn, axis=-1, keepdims=True)
    return xn * lax.rsqrt(ms + EPS) * fg


def _swiglu(h, wg, wu, wd):
    a = _dot(h, wg)
    u = _dot(h, wu)
    return _dot(((a * _sigmoid(a)) * u).astype(BF16), wd)


def _ffn_kernel(h_ref, x_ref, mod_ref, wg_ref, wu_ref, wd_ref, o_ref, acc_ref):
    e = pl.program_id(1)

    @pl.when(e == 0)
    def _():
        acc_ref[...] = jnp.zeros_like(acc_ref)

    acc_ref[...] += _swiglu(h_ref[...], wg_ref[...], wu_ref[...], wd_ref[...])

    @pl.when(e == pl.num_programs(1) - 1)
    def _():
        o_ref[...] = x_ref[...] + mod_ref[0, 5:6, :] * acc_ref[...]


def _ffn(h2, x, mods, wg, wu, wd, tm, rows_per_mod):
    m = x.shape[0]
    mod_map = (lambda i, e: (i // (rows_per_mod // tm), 0, 0)) if rows_per_mod else (lambda i, e: (0, 0, 0))
    row = lambda w: pl.BlockSpec((tm, w), lambda i, e: (i, 0))
    return pl.pallas_call(
        _ffn_kernel,
        out_shape=jax.ShapeDtypeStruct((m, D_MODEL), F32),
        grid=(m // tm, wg.shape[1] // D_FF_E),
        in_specs=[row(D_MODEL), row(D_MODEL), pl.BlockSpec((1, 6, D_MODEL), mod_map),
                  pl.BlockSpec((D_MODEL, D_FF_E), lambda i, e: (0, e)),
                  pl.BlockSpec((D_MODEL, D_FF_E), lambda i, e: (0, e)),
                  pl.BlockSpec((D_FF_E, D_MODEL), lambda i, e: (e, 0))],
        out_specs=row(D_MODEL),
        scratch_shapes=[pltpu.VMEM((tm, D_MODEL), F32)],
        compiler_params=_cparams(("arbitrary", "arbitrary")),
        name="ffn",
    )(h2, x, mods, wg, wu, wd)


ROUTE_TILE = 512
EXPERT_TILE = 512
DISPATCH_CHUNK = 512
COMBINE_TILE = 256
N_SLAB = D_MODEL // LANE


def _route_kernel(lg_ref, slab_ref, cnt_ref, carry_ref):
    @pl.when(pl.program_id(0) == 0)
    def _():
        carry_ref[...] = jnp.zeros_like(carry_ref)

    logits = lg_ref[...]
    tr = logits.shape[0]
    lane = lax.broadcasted_iota(jnp.int32, logits.shape, 1).astype(F32)
    neg = -jnp.inf
    l1 = jnp.where(lane < N_EXP, logits, neg)
    m1 = jnp.max(l1, axis=-1, keepdims=True)
    i1 = jnp.min(jnp.where(l1 == m1, lane, float(LANE)), axis=-1, keepdims=True)
    l2 = jnp.where(lane == i1, neg, l1)
    m2 = jnp.max(l2, axis=-1, keepdims=True)
    i2 = jnp.min(jnp.where(l2 == m2, lane, float(LANE)), axis=-1, keepdims=True)
    e2 = jnp.exp(m2 - m1)
    den = 1.0 + e2
    oh1 = jnp.where(lane == i1, 1.0, 0.0)
    oh2 = jnp.where(lane == i2, 1.0, 0.0)
    oh = oh1 + oh2
    r = lax.broadcasted_iota(jnp.int32, (tr, tr), 0)
    c = lax.broadcasted_iota(jnp.int32, (tr, tr), 1)
    before = jnp.where(c < r, 1.0, 0.0).astype(BF16)
    seen = _dot(before, oh.astype(BF16)) + carry_ref[...]
    r1 = jnp.sum(seen * oh1, axis=-1, keepdims=True)
    r2 = jnp.sum(seen * oh2, axis=-1, keepdims=True)
    carry_ref[...] += jnp.sum(oh, axis=0, keepdims=True)
    slab = jnp.zeros_like(logits)
    for j, v in enumerate((i1, i2, r1, r2, 1.0 / den, e2 / den)):
        slab = jnp.where(lane == float(j), v, slab)
    slab_ref[...] = slab
    cnt_ref[...] = jnp.broadcast_to(carry_ref[...], cnt_ref.shape)


def _route(logits):
    m = logits.shape[0]
    return pl.pallas_call(
        _route_kernel,
        out_shape=[jax.ShapeDtypeStruct((m, LANE), F32), jax.ShapeDtypeStruct((8, LANE), F32)],
        grid=(m // ROUTE_TILE,),
        in_specs=[pl.BlockSpec((ROUTE_TILE, LANE), lambda i: (i, 0))],
        out_specs=[pl.BlockSpec((ROUTE_TILE, LANE), lambda i: (i, 0)), pl.BlockSpec((8, LANE), lambda i: (0, 0))],
        scratch_shapes=[pltpu.VMEM((1, LANE), F32)],
        compiler_params=_cparams(("arbitrary",)),
        name="route",
    )(logits)


def _row_copy(src, dst, src_row, dst_row, sem):
    return pltpu.make_async_copy(src.at[src_row], dst.at[dst_row], sem)


def _rows_wait(src, dst, n, sem):
    pltpu.make_async_copy(src.at[pl.ds(0, n)], dst.at[pl.ds(0, n)], sem).wait()


def _dispatch_kernel(d0_ref, d1_ref, zs_ref, h_ref, xs_hbm, zbuf, sem, zsem):
    tc = DISPATCH_CHUNK
    base = pl.program_id(0) * tc

    @pl.when(pl.program_id(0) == 0)
    def _():
        zbuf[...] = jnp.zeros_like(zbuf)
        for j in range(2 * N_EXP):
            cp = pltpu.make_async_copy(zbuf, xs_hbm.at[pl.ds(zs_ref[j], EXPERT_TILE)], zsem)
            cp.start()
            cp.wait()

    def row(k, carry):
        _row_copy(h_ref, xs_hbm, k, d0_ref[base + k], sem).start()
        _row_copy(h_ref, xs_hbm, k, d1_ref[base + k], sem).start()
        return carry

    lax.fori_loop(0, tc, row, 0)
    _rows_wait(h_ref, xs_hbm, tc, sem)
    _rows_wait(h_ref, xs_hbm, tc, sem)


def _dispatch(dest0, dest1, zero_starts, h_slab, n_rows):
    m = h_slab.shape[0]
    tc = DISPATCH_CHUNK
    return pl.pallas_call(
        _dispatch_kernel,
        out_shape=jax.ShapeDtypeStruct((n_rows, N_SLAB, LANE), F32),
        grid_spec=pltpu.PrefetchScalarGridSpec(
            num_scalar_prefetch=3,
            grid=(m // tc,),
            in_specs=[pl.BlockSpec((tc, N_SLAB, LANE), lambda i, a, b, c: (i, 0, 0))],
            out_specs=pl.BlockSpec(memory_space=pl.ANY),
            scratch_shapes=[pltpu.VMEM((EXPERT_TILE, N_SLAB, LANE), F32),
                            pltpu.SemaphoreType.DMA, pltpu.SemaphoreType.DMA],
        ),
        compiler_params=_cparams(("arbitrary",)),
        name="dispatch",
    )(dest0, dest1, zero_starts, h_slab)


def _experts_kernel(te_ref, x_ref, wg_ref, wu_ref, wd_ref, y_ref):
    x = jnp.concatenate([x_ref[:, j, :] for j in range(N_SLAB)], axis=1).astype(BF16)
    y = _swiglu(x, wg_ref[...], wu_ref[...], wd_ref[...])
    for j in range(N_SLAB):
        y_ref[:, j, :] = y[:, j * LANE:(j + 1) * LANE]


def _experts(tile_expert, xs, wg, wu, wd):
    n_rows = xs.shape[0]
    slab = pl.BlockSpec((EXPERT_TILE, N_SLAB, LANE), lambda i, te: (i, 0, 0))
    return pl.pallas_call(
        _experts_kernel,
        out_shape=jax.ShapeDtypeStruct((n_rows, N_SLAB, LANE), F32),
        grid_spec=pltpu.PrefetchScalarGridSpec(
            num_scalar_prefetch=1,
            grid=(n_rows // EXPERT_TILE,),
            in_specs=[slab,
                      pl.BlockSpec((None, D_MODEL, D_FF_E), lambda i, te: (te[i], 0, 0)),
                      pl.BlockSpec((None, D_MODEL, D_FF_E), lambda i, te: (te[i], 0, 0)),
                      pl.BlockSpec((None, D_FF_E, D_MODEL), lambda i, te: (te[i], 0, 0))],
            out_specs=slab,
        ),
        compiler_params=_cparams(("arbitrary",)),
        name="experts",
    )(tile_expert, xs, wg, wu, wd)


def _combine_kernel(*refs, final):
    if final:
        d0_ref, d1_ref, x_ref, mod_ref, slab_ref, fg_ref, ys_hbm, o_ref, buf, sem = refs
    else:
        d0_ref, d1_ref, x_ref, mod_ref, slab_ref, ys_hbm, o_ref, buf, sem = refs
    tc = COMBINE_TILE
    base = pl.program_id(0) * tc

    def issue(k, carry):
        _row_copy(ys_hbm, buf.at[0], d0_ref[base + k], k, sem).start()
        _row_copy(ys_hbm, buf.at[1], d1_ref[base + k], k, sem).start()
        return carry

    lax.fori_loop(0, tc, issue, 0)
    _rows_wait(ys_hbm, buf.at[0], tc, sem)
    _rows_wait(ys_hbm, buf.at[1], tc, sem)
    slab = slab_ref[...]
    w1, w2 = slab[:, 4:5], slab[:, 5:6]
    y1 = jnp.concatenate([buf[0, :, j, :] for j in range(N_SLAB)], axis=1)
    y2 = jnp.concatenate([buf[1, :, j, :] for j in range(N_SLAB)], axis=1)
    xn = x_ref[...] + mod_ref[0, 5:6, :] * (w1 * y1 + w2 * y2)
    o_ref[...] = _final_norm(xn, fg_ref[...]) if final else xn


def _combine(dest0, dest1, x, mods, slab, ys, rows_per_mod, final_g=None):
    m = x.shape[0]
    tc = COMBINE_TILE
    final = final_g is not None
    mod_map = ((lambda i, a, b: (i // (rows_per_mod // tc), 0, 0)) if rows_per_mod
               else (lambda i, a, b: (0, 0, 0)))
    in_specs = [pl.BlockSpec((tc, D_MODEL), lambda i, a, b: (i, 0)),
                pl.BlockSpec((1, 6, D_MODEL), mod_map),
                pl.BlockSpec((tc, LANE), lambda i, a, b: (i, 0))]
    args = [x, mods, slab]
    if final:
        in_specs.append(pl.BlockSpec((1, D_MODEL), lambda i, a, b: (0, 0)))
        args.append(final_g)
    in_specs.append(pl.BlockSpec(memory_space=pl.ANY))
    args.append(ys)
    return pl.pallas_call(
        functools.partial(_combine_kernel, final=final),
        out_shape=jax.ShapeDtypeStruct((m, D_MODEL), F32),
        grid_spec=pltpu.PrefetchScalarGridSpec(
            num_scalar_prefetch=2,
            grid=(m // tc,),
            in_specs=in_specs,
            out_specs=pl.BlockSpec((tc, D_MODEL), lambda i, a, b: (i, 0)),
            scratch_shapes=[pltpu.VMEM((2, tc, N_SLAB, LANE), F32), pltpu.SemaphoreType.DMA],
        ),
        compiler_params=_cparams(("arbitrary",)),
        name="combine",
    )(dest0, dest1, *args)


def _moe(h_slab, logits, x, mods, wg, wu, wd, rows_per_mod, final_g=None):
    m = x.shape[0]
    t = EXPERT_TILE
    n_rows = TOP_K * m + N_EXP * t
    slab, counts = _route(logits)
    cnt = counts[0, :N_EXP].astype(jnp.int32)
    padded = (cnt + (t - 1)) // t * t
    ends = jnp.cumsum(padded)
    starts = ends - padded
    dest = jnp.take(starts, slab[:, 0:2].astype(jnp.int32)) + slab[:, 2:4].astype(jnp.int32)
    dest0, dest1 = dest[:, 0], dest[:, 1]
    tile_start = jnp.arange(n_rows // t, dtype=jnp.int32) * t
    tile_expert = jnp.minimum(jnp.sum(tile_start[:, None] >= ends[None, :], axis=1), N_EXP - 1).astype(jnp.int32)
    zero_starts = jnp.concatenate([
        jnp.maximum(ends - t, 0),
        jnp.minimum(ends[-1] + jnp.arange(N_EXP, dtype=jnp.int32) * t, n_rows - t)]).astype(jnp.int32)
    xs = _dispatch(dest0, dest1, zero_starts, h_slab, n_rows)
    ys = _experts(tile_expert, xs, wg, wu, wd)
    return _combine(dest0, dest1, x, mods, slab, ys, rows_per_mod, final_g)


def _prep_w_in(w):
    qa, ka, va, ub, qc = (w[:, 0:512], w[:, 512:1024], w[:, 1024:1536], w[:, 1536:2048], w[:, 2048:2560])
    kc, vc = w[:, 2560:2688], w[:, 2688:2816]
    dup = lambda a: jnp.concatenate([a[:, 0:64], a[:, 0:64], a[:, 64:128], a[:, 64:128]], axis=1)
    scale = LOG2E * DHA ** -0.5
    return jnp.concatenate([qa * scale, ka, va, qc * scale, dup(kc), dup(vc), ub], axis=1).astype(BF16)


def _rope_tables(t):
    rows = t // GRID_W
    row = jnp.repeat(jnp.arange(rows, dtype=F32), GRID_W)
    col = jnp.tile(jnp.arange(GRID_W, dtype=F32), rows)
    inv = ROPE_BASE ** (-jnp.arange(0, ROT_HALF, 2, dtype=F32) / ROT_HALF)
    ang_r, ang_c = row[:, None] * inv, col[:, None] * inv
    zero = jnp.zeros_like(ang_r)
    seg = lambda a, b: jnp.concatenate([a, b], axis=1)
    cos = jnp.concatenate([seg(jnp.cos(ang_r), jnp.cos(ang_r)), seg(jnp.cos(ang_c), jnp.cos(ang_c))], axis=1)
    s_next = jnp.concatenate([seg(-jnp.sin(ang_r), zero), seg(-jnp.sin(ang_c), zero)], axis=1)
    s_prev = jnp.concatenate([seg(zero, jnp.sin(ang_r)), seg(zero, jnp.sin(ang_c))], axis=1)
    two = lambda a: jnp.concatenate([a, a], axis=1)
    return two(cos), two(s_next), two(s_prev)


def kernel(x_prompt, x_sample, cache_diff_k, cache_diff_v, cache_win_k, cache_win_v, c, c_ctx, w_ada, b_ada, norm1_g, norm2_g, w_in, w_gate, lam_p, subln_g, w_pool, pool_scale, sink, w_branch, w_out, w_ff_gate, w_ff_up, w_ff_down, w_router, w_ex_gate, w_ex_up, w_ex_down, final_g):
    bp, s, _ = x_prompt.shape
    bs, t, _ = x_sample.shape
    past = cache_diff_k.shape[2]

    cvec = jnp.concatenate([c_ctx[None, :], c, jnp.zeros((MOD_ROWS - 1 - bs, D_MODEL), F32)], axis=0)
    mods = _adaln(cvec, w_ada, b_ada).reshape(DEPTH, MOD_ROWS, 6, D_MODEL)
    rope_tabs = _rope_tables(t)

    ck_a = cache_diff_k.reshape(bs, DEPTH, past, HA * 2 * DHA)
    cv_a = cache_diff_v.reshape(bs, DEPTH, past, HA * 2 * DHA)
    ck_c = cache_win_k.reshape(bs, DEPTH, past, KVC * DHC)
    cv_c = cache_win_v.reshape(bs, DEPTH, past, KVC * DHC)

    xc = x_prompt.reshape(bp * s, D_MODEL)
    xl = x_sample.reshape(bs * t, D_MODEL)
    caches = [[], [], [], []]
    for l in range(DEPTH):
        last = l == DEPTH - 1
        lam_init = 0.8 - 0.6 * math.exp(-0.3 * l)
        w_in_l = _prep_w_in(w_in[l])
        wgate_l = w_gate[l].astype(BF16)
        wpool_l = w_pool[l].astype(BF16)
        wbr_l = w_branch[l].astype(BF16)
        wout_l = w_out[l].astype(BF16)
        g1 = norm1_g[l][None, :]
        g2 = norm2_g[l][None, :]
        sg = subln_g[l][None, :]
        ps = pool_scale[l][None, :]
        fg = final_g[None, :] if last else None
        i = l // 2
        if l % 2 == 0:
            wr = None
            wg, wu, wd = (w_ff_gate[i].astype(BF16), w_ff_up[i].astype(BF16), w_ff_down[i].astype(BF16))
        else:
            wr_f = jnp.pad(w_router[i], ((0, 0), (0, LANE - N_EXP)))
            wr_hi = wr_f.astype(BF16)
            wr = (wr_hi, (wr_f - wr_hi.astype(F32)).astype(BF16))
            wg, wu, wd = (w_ex_gate[i].astype(BF16), w_ex_up[i].astype(BF16), w_ex_down[i].astype(BF16))

        def channel_mixer(res, mod, rows_per_mod):
            if wr is None:
                assert fg is None
                return _ffn(res[1], res[0], mod, wg, wu, wd, tm=512, rows_per_mod=rows_per_mod)
            return _moe(res[1], res[2], res[0], mod, wg, wu, wd, rows_per_mod, final_g=fg)

        m_c = mods[l, 0:1]
        pc, ubc, dk, dv, wk, wv = _inproj(xc, g1, m_c, w_in_l, tm=256, rows_per_mod=0)
        for lst, a in zip(caches, (dk, dv, wk, wv)):
            lst.append(a)
        oa = _diff_ctx(pc, lam_p[l], sg, lam_init, bp, s)
        pooled = _pool(ubc, bp, s)
        oc = _gqa_ctx(pc, sink[l], bp, s)
        res = _merge(xc, oa, pooled, oc, g1, g2, m_c, wgate_l, wpool_l, ps, wbr_l, wout_l,
                     tm=256, rows_per_mod=0, wr=wr)
        xc = channel_mixer(res, m_c, 0)

        m_l = mods[l, 1:1 + bs]
        pl_, ubl = _inproj(xl, g1, m_l, w_in_l, tm=256, rows_per_mod=t, rope_tabs=rope_tabs)
        oa = _diff_lat(pl_, ck_a, cv_a, l, lam_p[l], sg, lam_init, bs, t, tq=128)
        pooled = _pool(ubl, bs, t)
        oc = _gqa_lat(pl_, ck_c, cv_c, l, sink[l], bs, t)
        res = _merge(xl, oa, pooled, oc, g1, g2, m_l, wgate_l, wpool_l, ps, wbr_l, wout_l,
                     tm=256, rows_per_mod=t, wr=wr)
        xl = channel_mixer(res, m_l, t)

    y_prompt = xc.reshape(bp, s, D_MODEL)
    y_sample = xl.reshape(bs, t, D_MODEL)
    new_diff_k = jnp.stack(caches[0], axis=0).reshape(DEPTH, bp, s, HA, 2, DHA).swapaxes(0, 1)
    new_diff_v = jnp.stack(caches[1], axis=0).reshape(DEPTH, bp, s, HA, 2 * DHA).swapaxes(0, 1)
    new_win_k = jnp.stack(caches[2], axis=0).reshape(DEPTH, bp, s, KVC, DHC).swapaxes(0, 1)
    new_win_v = jnp.stack(caches[3], axis=0).reshape(DEPTH, bp, s, KVC, DHC).swapaxes(0, 1)
    return (y_prompt, y_sample, new_diff_k, new_diff_v, new_win_k, new_win_v)
```

```python
import functools
import math

import numpy as np
import jax
import jax.numpy as jnp
from jax import lax
from jax.experimental import pallas as pl
from jax.experimental.pallas import tpu as pltpu

F32 = jnp.float32
BF16 = jnp.bfloat16

D_MODEL = 1024
DEPTH = 2
GRID_W = 64
HA = 4
DHA = 64
HC = 8
KVC = 2
REPC = HC // KVC
DHC = 64
WINDOW = 128
QBLK = 128
POOL_WINDOWS = (2, 4, 8, 16)
POOL_GC = 128
POOL_W = 512
BRANCH_W = 512
N_BRANCH = 3
ROPE_BASE = 10000.0
ROT_HALF = 32
N_EXP = 8
TOP_K = 2
D_FF_E = 1408
EPS = 1e-6
NEG_INF = -1e30
LOG2E = math.log2(math.e)

LANE = 128
MOD_ROWS = 16
P_W = 20 * LANE
IN_W2 = 24 * LANE
VMEM_LIMIT = 56 * 1024 * 1024


def _cparams(sem):
    return pltpu.CompilerParams(dimension_semantics=sem, vmem_limit_bytes=VMEM_LIMIT)


def _dot(a, b):
    return jnp.dot(a, b, preferred_element_type=F32)


def _dot_nt(a, b):
    return lax.dot_general(a, b, (((1,), (1,)), ((), ())), preferred_element_type=F32)


def _sigmoid(x):
    return 1.0 / (1.0 + jnp.exp(-x))


def _modnorm(x, g, shift, scale):
    ms = jnp.mean(x * x, axis=-1, keepdims=True)
    return (x * lax.rsqrt(ms + EPS) * g) * (1.0 + scale) + shift


def _adaln_kernel(c_ref, w_ref, b_ref, o_ref):
    c = c_ref[...]
    s = c * _sigmoid(c)
    o_ref[0] = _dot(s.astype(BF16), w_ref[0].astype(BF16)) + b_ref[0]


def _adaln(cvec, w_ada, b_ada):
    tn = 1536
    n = 6 * D_MODEL
    return pl.pallas_call(
        _adaln_kernel,
        out_shape=jax.ShapeDtypeStruct((DEPTH, MOD_ROWS, n), F32),
        grid=(DEPTH, n // tn),
        in_specs=[
            pl.BlockSpec((MOD_ROWS, D_MODEL), lambda l, j: (0, 0)),
            pl.BlockSpec((1, D_MODEL, tn), lambda l, j: (l, 0, j)),
            pl.BlockSpec((1, 1, tn), lambda l, j: (l, 0, j)),
        ],
        out_specs=pl.BlockSpec((1, MOD_ROWS, tn), lambda l, j: (l, 0, j)),
        compiler_params=_cparams(("arbitrary", "arbitrary")),
        name="adaln",
    )(cvec, w_ada, b_ada.reshape(DEPTH, 1, n))


def _rope(p, cos, sin_next, sin_prev):
    nxt = pltpu.roll(p, LANE - 16, 1)
    prv = pltpu.roll(p, 16, 1)
    return p * cos + nxt * sin_next + prv * sin_prev


def _inproj_kernel(*refs, rope):
    if rope:
        (x_ref, g_ref, mod_ref, w_ref, cos_ref, sn_ref, sp_ref, p_ref, ub_ref) = refs
    else:
        (x_ref, g_ref, mod_ref, w_ref, p_ref, ub_ref, dk_ref, dv_ref, wk_ref, wv_ref) = refs
    h = _modnorm(x_ref[...], g_ref[...], mod_ref[0, 0:1, :], mod_ref[0, 1:2, :])
    hb = h.astype(BF16)
    if rope:
        cos, sn, sp = cos_ref[...], sn_ref[...], sp_ref[...]
    lane = lax.broadcasted_iota(jnp.int32, (1, LANE), 1)
    for c in range(IN_W2 // 512):
        pc = _dot(hb, w_ref[:, c * 512:(c + 1) * 512])
        blocks = [pc[:, j * LANE:(j + 1) * LANE] for j in range(4)]
        if c == 5:
            ub_ref[...] = pc
            continue
        if rope:
            n_rot = {0: 4, 1: 4, 2: 0, 3: 4, 4: 2}[c]
            blocks = [_rope(b, cos, sn, sp) if j < n_rot else b for j, b in enumerate(blocks)]
        elif c == 1:
            dk_ref[...] = pc
        elif c == 2:
            dv_ref[...] = pc
        elif c == 4:
            wk_ref[...] = jnp.where(lane < DHC, blocks[0], blocks[1])
            wv_ref[...] = jnp.where(lane < DHC, blocks[2], blocks[3])
        for j in range(4):
            p_ref[:, (4 * c + j) * LANE:(4 * c + j + 1) * LANE] = blocks[j].astype(BF16)


def _inproj(x, g, mods, w, tm, rows_per_mod, rope_tabs=None):
    m = x.shape[0]
    rope = rope_tabs is not None
    mod_map = (lambda i: (i // (rows_per_mod // tm), 0, 0)) if rows_per_mod else (lambda i: (0, 0, 0))
    in_specs = [
        pl.BlockSpec((tm, D_MODEL), lambda i: (i, 0)),
        pl.BlockSpec((1, D_MODEL), lambda i: (0, 0)),
        pl.BlockSpec((1, 6, D_MODEL), mod_map),
        pl.BlockSpec((D_MODEL, IN_W2), lambda i: (0, 0)),
    ]
    args = [x, g, mods, w]
    out_shape = [jax.ShapeDtypeStruct((m, P_W), BF16), jax.ShapeDtypeStruct((m, POOL_W), F32)]
    out_specs = [pl.BlockSpec((tm, P_W), lambda i: (i, 0)), pl.BlockSpec((tm, POOL_W), lambda i: (i, 0))]
    if rope:
        t = rope_tabs[0].shape[0]
        for tab in rope_tabs:
            in_specs.append(pl.BlockSpec((tm, LANE), lambda i: (i % (t // tm), 0)))
            args.append(tab)
    else:
        for wdt in (512, 512, LANE, LANE):
            out_shape.append(jax.ShapeDtypeStruct((m, wdt), F32))
            out_specs.append(pl.BlockSpec((tm, wdt), lambda i: (i, 0)))
    return pl.pallas_call(
        functools.partial(_inproj_kernel, rope=rope),
        out_shape=out_shape,
        grid=(m // tm,),
        in_specs=in_specs,
        out_specs=out_specs,
        compiler_params=_cparams(("arbitrary",)),
        name="inproj_rope" if rope else "inproj",
    )(*args)


def _diff_lambda(lam_ref, lam_init):
    lf = lam_ref[...]
    s1 = jnp.sum(lf[0:1] * lf[1:2], axis=-1, keepdims=True)
    s2 = jnp.sum(lf[2:3] * lf[3:4], axis=-1, keepdims=True)
    return jnp.exp(s1) - jnp.exp(s2) + lam_init


def _diff_scores(q, k):
    lane = lax.broadcasted_iota(jnp.int32, (1, LANE), 1)
    zero = jnp.zeros_like(q)
    q2 = jnp.concatenate([jnp.where(lane < DHA, q, zero), jnp.where(lane < DHA, zero, q)], axis=0)
    return _dot_nt(q2, k)


def _diff_finish(s, v, lam, g, lam_init):
    tq = s.shape[0] // 2
    e = jnp.exp2(s - jnp.max(s, axis=-1, keepdims=True))
    inv = 1.0 / jnp.sum(e, axis=-1, keepdims=True)
    o2 = _dot(e.astype(BF16), v) * inv
    o = o2[:tq] - lam * o2[tq:]
    ms = jnp.mean(o * o, axis=-1, keepdims=True)
    return ((o * lax.rsqrt(ms + EPS) * g) * (1.0 - lam_init)).astype(BF16)


def _diff_tile(q, k, v, lam, g, lam_init):
    return _diff_finish(_diff_scores(q, k), v, lam, g, lam_init)


def _diff_ctx_kernel(q_ref, k_ref, v_ref, lam_ref, g_ref, o_ref, *, lam_init):
    lam = _diff_lambda(lam_ref, lam_init)
    g = g_ref[...]
    for h in range(HA):
        sl = slice(h * LANE, (h + 1) * LANE)
        o_ref[:, sl] = _diff_tile(q_ref[:, sl], k_ref[:, sl], v_ref[:, sl], lam, g, lam_init)


def _diff_ctx(p, lam_p, subln_g, lam_init, nb, s):
    w = HA * LANE
    return pl.pallas_call(
        functools.partial(_diff_ctx_kernel, lam_init=lam_init),
        out_shape=jax.ShapeDtypeStruct((nb * s, w), BF16),
        grid=(nb,),
        in_specs=[
            pl.BlockSpec((s, w), lambda b: (b, 0)),
            pl.BlockSpec((s, w), lambda b: (b, 1)),
            pl.BlockSpec((s, w), lambda b: (b, 2)),
            pl.BlockSpec((4, DHA), lambda b: (0, 0)),
            pl.BlockSpec((1, LANE), lambda b: (0, 0)),
        ],
        out_specs=pl.BlockSpec((s, w), lambda b: (b, 0)),
        compiler_params=_cparams(("arbitrary",)),
        name="diff_ctx",
    )(p, p, p, lam_p, subln_g)


def _diff_lat_kernel(q_ref, kn_ref, vn_ref, kc_ref, vc_ref, lam_ref, g_ref, o_ref,
                     k_all, v_all, s_even, s_odd, *, lam_init, past, tq):
    k_all[0:past, :] = kc_ref[0, 0].astype(BF16)
    v_all[0:past, :] = vc_ref[0, 0].astype(BF16)
    k_all[past:, :] = kn_ref[...]
    v_all[past:, :] = vn_ref[...]
    lam = _diff_lambda(lam_ref, lam_init)
    g = g_ref[...]
    n_tiles = q_ref.shape[0] // tq

    def rows(i):
        return pl.ds(pl.multiple_of(i * tq, tq), tq)

    s_even[...] = _diff_scores(q_ref[rows(0), :], k_all[...])

    def body(j, carry):
        i = 2 * j
        s_odd[...] = _diff_scores(q_ref[rows(i + 1), :], k_all[...])
        o_ref[rows(i), :] = _diff_finish(s_even[...], v_all[...], lam, g, lam_init)
        s_even[...] = _diff_scores(q_ref[rows(jnp.minimum(i + 2, n_tiles - 1)), :], k_all[...])
        o_ref[rows(i + 1), :] = _diff_finish(s_odd[...], v_all[...], lam, g, lam_init)
        return carry

    lax.fori_loop(0, n_tiles // 2, body, 0)


def _diff_lat(p, cache_k, cache_v, l, lam_p, subln_g, lam_init, nb, t, tq):
    past = cache_k.shape[2]
    return pl.pallas_call(
        functools.partial(_diff_lat_kernel, lam_init=lam_init, past=past, tq=tq),
        out_shape=jax.ShapeDtypeStruct((nb * t, HA * LANE), BF16),
        grid=(nb, HA),
        in_specs=[
            pl.BlockSpec((t, LANE), lambda b, h: (b, h)),
            pl.BlockSpec((t, LANE), lambda b, h: (b, 4 + h)),
            pl.BlockSpec((t, LANE), lambda b, h: (b, 8 + h)),
            pl.BlockSpec((1, 1, past, LANE), lambda b, h: (b, l, 0, h)),
            pl.BlockSpec((1, 1, past, LANE), lambda b, h: (b, l, 0, h)),
            pl.BlockSpec((4, DHA), lambda b, h: (0, 0)),
            pl.BlockSpec((1, LANE), lambda b, h: (0, 0)),
        ],
        out_specs=pl.BlockSpec((t, LANE), lambda b, h: (b, h)),
        scratch_shapes=[pltpu.VMEM((past + t, LANE), BF16), pltpu.VMEM((past + t, LANE), BF16),
                        pltpu.VMEM((2 * tq, past + t), F32), pltpu.VMEM((2 * tq, past + t), F32)],
        compiler_params=_cparams(("arbitrary", "arbitrary")),
        name="diff_lat",
    )(p, p, p, cache_k, cache_v, lam_p, subln_g)


POOL_PAD = 16


def _pool_kernel(u_ref, o_ref, pad_ref, *, t):
    tp = t + POOL_PAD
    pos = lax.broadcasted_iota(jnp.int32, (t, 1), 0)
    for gi, w in enumerate(POOL_WINDOWS):
        sl = slice(gi * POOL_GC, (gi + 1) * POOL_GC)
        u = u_ref[0, :, sl]
        pad_ref[0:t, :] = u
        pad_ref[t:tp, :] = jnp.zeros((POOL_PAD, POOL_GC), F32)
        acc = pad_ref[...]
        step = 1
        while step < w:
            acc = acc + pltpu.roll(acc, step, 0)
            step *= 2
        ahead = w - w // 2 - 1
        if ahead:
            acc = pltpu.roll(acc, tp - ahead, 0)
        lo = jnp.maximum(pos - w // 2, 0)
        hi = jnp.minimum(pos + (w - w // 2), t)
        mean = acc[0:t] / (hi - lo).astype(F32)
        o_ref[0, :, sl] = (mean - u).astype(BF16)


def _pool(ub, nb, t):
    return pl.pallas_call(
        functools.partial(_pool_kernel, t=t),
        out_shape=jax.ShapeDtypeStruct((nb, t, POOL_W), BF16),
        grid=(nb,),
        in_specs=[pl.BlockSpec((1, t, POOL_W), lambda b: (b, 0, 0))],
        out_specs=pl.BlockSpec((1, t, POOL_W), lambda b: (b, 0, 0)),
        scratch_shapes=[pltpu.VMEM((t + POOL_PAD, POOL_GC), F32)],
        compiler_params=_cparams(("arbitrary",)),
        name="pool",
    )(ub.reshape(nb, t, POOL_W)).reshape(nb * t, POOL_W)


def _stack_heads(q):
    lane = lax.broadcasted_iota(jnp.int32, (1, LANE), 1)
    zero = jnp.zeros_like(q[:, :LANE])
    parts = []
    for pair in range(2):
        qp = q[:, pair * LANE:(pair + 1) * LANE]
        parts.append(jnp.where(lane < DHC, qp, zero))
        parts.append(jnp.where(lane < DHC, zero, qp))
    return jnp.concatenate(parts, axis=0)


def _gqa_core(score_parts, masks, value_parts, sink_ref, g, tq):
    lane = lax.broadcasted_iota(jnp.int32, (1, LANE), 1)
    e_parts = [[] for _ in score_parts]
    invs = []
    for r in range(REPC):
        rows = slice(r * tq, (r + 1) * tq)
        sink = sink_ref[g * REPC + r] * LOG2E
        parts = [s[rows] if mk is None else jnp.where(mk, s[rows], NEG_INF)
                 for s, mk in zip(score_parts, masks)]
        m = jnp.maximum(functools.reduce(jnp.maximum, [jnp.max(s, axis=-1, keepdims=True) for s in parts]), sink)
        es = [jnp.exp2(s - m) for s in parts]
        den = functools.reduce(lambda a, b: a + b, [jnp.sum(e, axis=-1, keepdims=True) for e in es])
        invs.append(1.0 / (den + jnp.exp2(sink - m)))
        for lst, e in zip(e_parts, es):
            lst.append(e.astype(BF16))
    o = functools.reduce(lambda a, b: a + b,
                         [_dot(jnp.concatenate(lst, axis=0), v) for lst, v in zip(e_parts, value_parts)])
    o = o * jnp.concatenate(invs, axis=0)
    outs = [o[r * tq:(r + 1) * tq] for r in range(REPC)]
    return jnp.concatenate([jnp.where(lane < DHC, outs[0], outs[1]),
                            jnp.where(lane < DHC, outs[2], outs[3])], axis=1).astype(BF16)


def _gqa_ctx_kernel(sink_ref, q_ref, k_ref, v_ref, o_ref, *, s):
    g = pl.program_id(1)
    q4 = _stack_heads(q_ref[...])
    sc = _dot_nt(q4, k_ref[...])
    o_ref[...] = _gqa_core([sc], [None], [v_ref[...]], sink_ref, g, s)


def _gqa_ctx(p, sink, nb, s):
    return pl.pallas_call(
        functools.partial(_gqa_ctx_kernel, s=s),
        out_shape=jax.ShapeDtypeStruct((nb * s, HC * DHC), BF16),
        grid=(nb, KVC),
        in_specs=[
            pl.BlockSpec(memory_space=pltpu.SMEM),
            pl.BlockSpec((s, 2 * LANE), lambda b, g: (b, 6 + g)),
            pl.BlockSpec((s, LANE), lambda b, g: (b, 16 + g)),
            pl.BlockSpec((s, LANE), lambda b, g: (b, 18 + g)),
        ],
        out_specs=pl.BlockSpec((s, 2 * LANE), lambda b, g: (b, g)),
        compiler_params=_cparams(("arbitrary", "arbitrary")),
        name="gqa_ctx",
    )(sink, p, p, p)


def _gqa_lat_kernel(sink_ref, q_ref, kn_ref, vn_ref, kc_ref, vc_ref, o_ref, kc_dup, vc_dup,
                    sc_even, sw_even, sc_odd, sw_odd, *, t):
    g = pl.program_id(1)
    lane = lax.broadcasted_iota(jnp.int32, (1, LANE), 1)
    keep = jnp.where(lane < DHC, 0, 1) == g
    for src, dst in ((kc_ref, kc_dup), (vc_ref, vc_dup)):
        blk = src[0, 0]
        dst[...] = jnp.where(keep, blk, pltpu.roll(blk, DHC, 1)).astype(BF16)
    span = QBLK + 2 * WINDOW
    n_blocks = t // QBLK

    def rows(i):
        return pl.ds(pl.multiple_of(i * QBLK, QBLK), QBLK)

    def win_start(i):
        return pl.multiple_of(jnp.clip((i - 1) * QBLK, 0, t - span), QBLK)

    def scores(i, sc_ref, sw_ref):
        q4 = _stack_heads(q_ref[rows(i), :])
        sc_ref[...] = _dot_nt(q4, kc_dup[...])
        sw_ref[...] = _dot_nt(q4, kn_ref[pl.ds(win_start(i), span), :])

    def finish(i, sc_ref, sw_ref):
        w0 = win_start(i)
        qpos = i * QBLK + lax.broadcasted_iota(jnp.int32, (QBLK, span), 0)
        kpos = w0 + lax.broadcasted_iota(jnp.int32, (QBLK, span), 1)
        band = jnp.abs(qpos - kpos) <= WINDOW
        o_ref[rows(i), :] = _gqa_core([sc_ref[...], sw_ref[...]], [None, band],
                                      [vc_dup[...], vn_ref[pl.ds(w0, span), :]], sink_ref, g, QBLK)

    scores(0, sc_even, sw_even)

    def body(j, carry):
        i = 2 * j
        scores(i + 1, sc_odd, sw_odd)
        finish(i, sc_even, sw_even)
        scores(jnp.minimum(i + 2, n_blocks - 1), sc_even, sw_even)
        finish(i + 1, sc_odd, sw_odd)
        return carry

    lax.fori_loop(0, n_blocks // 2, body, 0)


def _gqa_lat(p, cache_k, cache_v, l, sink, nb, t):
    past = cache_k.shape[2]
    return pl.pallas_call(
        functools.partial(_gqa_lat_kernel, t=t),
        out_shape=jax.ShapeDtypeStruct((nb * t, HC * DHC), BF16),
        grid=(nb, KVC),
        in_specs=[
            pl.BlockSpec(memory_space=pltpu.SMEM),
            pl.BlockSpec((t, 2 * LANE), lambda b, g: (b, 6 + g)),
            pl.BlockSpec((t, LANE), lambda b, g: (b, 16 + g)),
            pl.BlockSpec((t, LANE), lambda b, g: (b, 18 + g)),
            pl.BlockSpec((1, 1, past, LANE), lambda b, g: (b, l, 0, 0)),
            pl.BlockSpec((1, 1, past, LANE), lambda b, g: (b, l, 0, 0)),
        ],
        out_specs=pl.BlockSpec((t, 2 * LANE), lambda b, g: (b, g)),
        scratch_shapes=[pltpu.VMEM((past, LANE), BF16), pltpu.VMEM((past, LANE), BF16)]
        + [pltpu.VMEM((REPC * QBLK, past), F32), pltpu.VMEM((REPC * QBLK, QBLK + 2 * WINDOW), F32)] * 2,
        compiler_params=_cparams(("arbitrary", "arbitrary")),
        name="gqa_lat",
    )(sink, p, p, p, cache_k, cache_v)


def _merge_kernel(*refs, router):
    (x_ref, oa_ref, pb_ref, oc_ref, g1_ref, g2_ref, mod_ref, wgate_ref, wpool_ref, ps_ref,
     wbr_ref, wout_ref) = refs[:12]
    if router:
        wr_hi_ref, wr_lo_ref, xo_ref, h2_ref, lg_ref = refs[12:]
    else:
        xo_ref, h2_ref = refs[12:]
    x = x_ref[...]
    h = _modnorm(x, g1_ref[...], mod_ref[0, 0:1, :], mod_ref[0, 1:2, :])
    hb = h.astype(BF16)
    pooled = pb_ref[...]
    yb = jnp.concatenate(
        [_dot(pooled[:, j * POOL_GC:(j + 1) * POOL_GC], wpool_ref[j]) for j in range(len(POOL_WINDOWS))],
        axis=1) * ps_ref[...]
    branches = (oa_ref[...], yb.astype(BF16), oc_ref[...])
    acc = None
    for n in range(N_BRANCH):
        gate = _sigmoid(_dot(hb, wgate_ref[:, n * D_MODEL:(n + 1) * D_MODEL]))
        term = gate * _dot(branches[n], wbr_ref[n])
        acc = term if acc is None else acc + term
    xn = x + mod_ref[0, 2:3, :] * _dot(acc.astype(BF16), wout_ref[...])
    xo_ref[...] = xn
    h2 = _modnorm(xn, g2_ref[...], mod_ref[0, 3:4, :], mod_ref[0, 4:5, :])
    h2_hi = h2.astype(BF16)
    if router:
        for j in range(D_MODEL // LANE):
            h2_ref[:, j, :] = h2[:, j * LANE:(j + 1) * LANE]
        h2_lo = (h2 - h2_hi.astype(F32)).astype(BF16)
        w_hi, w_lo = wr_hi_ref[...], wr_lo_ref[...]
        lg_ref[...] = (_dot(h2_hi, w_hi) + (_dot(h2_hi, w_lo) + _dot(h2_lo, w_hi))) + _dot(h2_lo, w_lo)
    else:
        h2_ref[...] = h2_hi


def _merge(x, oa, pooled, oc, g1, g2, mods, wgate, wpool, pscale, wbr, wout, tm, rows_per_mod, wr=None):
    m = x.shape[0]
    router = wr is not None
    mod_map = (lambda i: (i // (rows_per_mod // tm), 0, 0)) if rows_per_mod else (lambda i: (0, 0, 0))
    row = lambda w: pl.BlockSpec((tm, w), lambda i: (i, 0))
    full = lambda a: pl.BlockSpec(a.shape, lambda i: (0,) * a.ndim)
    in_specs = [row(D_MODEL), row(BRANCH_W), row(BRANCH_W), row(BRANCH_W), full(g1), full(g2),
                pl.BlockSpec((1, 6, D_MODEL), mod_map), full(wgate), full(wpool), full(pscale),
                full(wbr), full(wout)]
    args = [x, oa, pooled, oc, g1, g2, mods, wgate, wpool, pscale, wbr, wout]
    out_shape = [jax.ShapeDtypeStruct((m, D_MODEL), F32)]
    out_specs = [row(D_MODEL)]
    if router:
        in_specs += [full(wr[0]), full(wr[1])]
        args += [wr[0], wr[1]]
        out_shape += [jax.ShapeDtypeStruct((m, D_MODEL // LANE, LANE), F32), jax.ShapeDtypeStruct((m, LANE), F32)]
        out_specs += [pl.BlockSpec((tm, D_MODEL // LANE, LANE), lambda i: (i, 0, 0)), row(LANE)]
    else:
        out_shape.append(jax.ShapeDtypeStruct((m, D_MODEL), BF16))
        out_specs.append(row(D_MODEL))
    return pl.pallas_call(
        functools.partial(_merge_kernel, router=router),
        out_shape=out_shape,
        grid=(m // tm,),
        in_specs=in_specs,
        out_specs=out_specs,
        compiler_params=_cparams(("arbitrary",)),
        name="merge",
    )(*args)


def _final_norm(xn, fg):
    ms = jnp.mean(xn * xn, axis=-1, keepdims=True)
    return xn * lax.rsqrt(ms + EPS) * fg


def _swiglu(h, wg, wu, wd):
    a = _dot(h, wg)
    u = _dot(h, wu)
    return _dot(((a * _sigmoid(a)) * u).astype(BF16), wd)


def _ffn_kernel(h_ref, x_ref, mod_ref, wg_ref, wu_ref, wd_ref, o_ref, acc_ref):
    e = pl.program_id(1)

    @pl.when(e == 0)
    def _():
        acc_ref[...] = jnp.zeros_like(acc_ref)

    acc_ref[...] += _swiglu(h_ref[...], wg_ref[...], wu_ref[...], wd_ref[...])

    @pl.when(e == pl.num_programs(1) - 1)
    def _():
        o_ref[...] = x_ref[...] + mod_ref[0, 5:6, :] * acc_ref[...]


def _ffn(h2, x, mods, wg, wu, wd, tm, rows_per_mod):
    m = x.shape[0]
    mod_map = (lambda i, e: (i // (rows_per_mod // tm), 0, 0)) if rows_per_mod else (lambda i, e: (0, 0, 0))
    row = lambda w: pl.BlockSpec((tm, w), lambda i, e: (i, 0))
    return pl.pallas_call(
        _ffn_kernel,
        out_shape=jax.ShapeDtypeStruct((m, D_MODEL), F32),
        grid=(m // tm, wg.shape[1] // D_FF_E),
        in_specs=[row(D_MODEL), row(D_MODEL), pl.BlockSpec((1, 6, D_MODEL), mod_map),
                  pl.BlockSpec((D_MODEL, D_FF_E), lambda i, e: (0, e)),
                  pl.BlockSpec((D_MODEL, D_FF_E), lambda i, e: (0, e)),
                  pl.BlockSpec((D_FF_E, D_MODEL), lambda i, e: (e, 0))],
        out_specs=row(D_MODEL),
        scratch_shapes=[pltpu.VMEM((tm, D_MODEL), F32)],
        compiler_params=_cparams(("arbitrary", "arbitrary")),
        name="ffn",
    )(h2, x, mods, wg, wu, wd)


ROUTE_TILE = 512
EXPERT_TILE = 512
DISPATCH_CHUNK = 512
COMBINE_TILE = 256
N_SLAB = D_MODEL // LANE


def _route_kernel(lg_ref, slab_ref, cnt_ref, carry_ref):
    @pl.when(pl.program_id(0) == 0)
    def _():
        carry_ref[...] = jnp.zeros_like(carry_ref)

    logits = lg_ref[...]
    tr = logits.shape[0]
    lane = lax.broadcasted_iota(jnp.int32, logits.shape, 1).astype(F32)
    neg = -jnp.inf
    l1 = jnp.where(lane < N_EXP, logits, neg)
    m1 = jnp.max(l1, axis=-1, keepdims=True)
    i1 = jnp.min(jnp.where(l1 == m1, lane, float(LANE)), axis=-1, keepdims=True)
    l2 = jnp.where(lane == i1, neg, l1)
    m2 = jnp.max(l2, axis=-1, keepdims=True)
    i2 = jnp.min(jnp.where(l2 == m2, lane, float(LANE)), axis=-1, keepdims=True)
    e2 = jnp.exp(m2 - m1)
    den = 1.0 + e2
    oh1 = jnp.where(lane == i1, 1.0, 0.0)
    oh2 = jnp.where(lane == i2, 1.0, 0.0)
    oh = oh1 + oh2
    r = lax.broadcasted_iota(jnp.int32, (tr, tr), 0)
    c = lax.broadcasted_iota(jnp.int32, (tr, tr), 1)
    before = jnp.where(c < r, 1.0, 0.0).astype(BF16)
    seen = _dot(before, oh.astype(BF16)) + carry_ref[...]
    r1 = jnp.sum(seen * oh1, axis=-1, keepdims=True)
    r2 = jnp.sum(seen * oh2, axis=-1, keepdims=True)
    carry_ref[...] += jnp.sum(oh, axis=0, keepdims=True)
    slab = jnp.zeros_like(logits)
    for j, v in enumerate((i1, i2, r1, r2, 1.0 / den, e2 / den)):
        slab = jnp.where(lane == float(j), v, slab)
    slab_ref[...] = slab
    cnt_ref[...] = jnp.broadcast_to(carry_ref[...], cnt_ref.shape)


def _route(logits):
    m = logits.shape[0]
    return pl.pallas_call(
        _route_kernel,
        out_shape=[jax.ShapeDtypeStruct((m, LANE), F32), jax.ShapeDtypeStruct((8, LANE), F32)],
        grid=(m // ROUTE_TILE,),
        in_specs=[pl.BlockSpec((ROUTE_TILE, LANE), lambda i: (i, 0))],
        out_specs=[pl.BlockSpec((ROUTE_TILE, LANE), lambda i: (i, 0)), pl.BlockSpec((8, LANE), lambda i: (0, 0))],
        scratch_shapes=[pltpu.VMEM((1, LANE), F32)],
        compiler_params=_cparams(("arbitrary",)),
        name="route",
    )(logits)


def _row_copy(src, dst, src_row, dst_row, sem):
    return pltpu.make_async_copy(src.at[src_row], dst.at[dst_row], sem)


def _rows_wait(src, dst, n, sem):
    pltpu.make_async_copy(src.at[pl.ds(0, n)], dst.at[pl.ds(0, n)], sem).wait()


def _dispatch_kernel(d0_ref, d1_ref, zs_ref, h_ref, xs_hbm, zbuf, sem, zsem):
    tc = DISPATCH_CHUNK
    base = pl.program_id(0) * tc

    @pl.when(pl.program_id(0) == 0)
    def _():
        zbuf[...] = jnp.zeros_like(zbuf)
        for j in range(2 * N_EXP):
            cp = pltpu.make_async_copy(zbuf, xs_hbm.at[pl.ds(zs_ref[j], EXPERT_TILE)], zsem)
            cp.start()
            cp.wait()

    def row(k, carry):
        _row_copy(h_ref, xs_hbm, k, d0_ref[base + k], sem).start()
        _row_copy(h_ref, xs_hbm, k, d1_ref[base + k], sem).start()
        return carry

    lax.fori_loop(0, tc, row, 0)
    _rows_wait(h_ref, xs_hbm, tc, sem)
    _rows_wait(h_ref, xs_hbm, tc, sem)


def _dispatch(dest0, dest1, zero_starts, h_slab, n_rows):
    m = h_slab.shape[0]
    tc = DISPATCH_CHUNK
    return pl.pallas_call(
        _dispatch_kernel,
        out_shape=jax.ShapeDtypeStruct((n_rows, N_SLAB, LANE), F32),
        grid_spec=pltpu.PrefetchScalarGridSpec(
            num_scalar_prefetch=3,
            grid=(m // tc,),
            in_specs=[pl.BlockSpec((tc, N_SLAB, LANE), lambda i, a, b, c: (i, 0, 0))],
            out_specs=pl.BlockSpec(memory_space=pl.ANY),
            scratch_shapes=[pltpu.VMEM((EXPERT_TILE, N_SLAB, LANE), F32),
                            pltpu.SemaphoreType.DMA, pltpu.SemaphoreType.DMA],
        ),
        compiler_params=_cparams(("arbitrary",)),
        name="dispatch",
    )(dest0, dest1, zero_starts, h_slab)


def _experts_kernel(te_ref, x_ref, wg_ref, wu_ref, wd_ref, y_ref):
    x = jnp.concatenate([x_ref[:, j, :] for j in range(N_SLAB)], axis=1).astype(BF16)
    y = _swiglu(x, wg_ref[...], wu_ref[...], wd_ref[...])
    for j in range(N_SLAB):
        y_ref[:, j, :] = y[:, j * LANE:(j + 1) * LANE]


def _experts(tile_expert, xs, wg, wu, wd):
    n_rows = xs.shape[0]
    slab = pl.BlockSpec((EXPERT_TILE, N_SLAB, LANE), lambda i, te: (i, 0, 0))
    return pl.pallas_call(
        _experts_kernel,
        out_shape=jax.ShapeDtypeStruct((n_rows, N_SLAB, LANE), F32),
        grid_spec=pltpu.PrefetchScalarGridSpec(
            num_scalar_prefetch=1,
            grid=(n_rows // EXPERT_TILE,),
            in_specs=[slab,
                      pl.BlockSpec((None, D_MODEL, D_FF_E), lambda i, te: (te[i], 0, 0)),
                      pl.BlockSpec((None, D_MODEL, D_FF_E), lambda i, te: (te[i], 0, 0)),
                      pl.BlockSpec((None, D_FF_E, D_MODEL), lambda i, te: (te[i], 0, 0))],
            out_specs=slab,
        ),
        compiler_params=_cparams(("arbitrary",)),
        name="experts",
    )(tile_expert, xs, wg, wu, wd)


def _combine_kernel(*refs, final):
    if final:
        d0_ref, d1_ref, x_ref, mod_ref, slab_ref, fg_ref, ys_hbm, o_ref, buf, sem = refs
    else:
        d0_ref, d1_ref, x_ref, mod_ref, slab_ref, ys_hbm, o_ref, buf, sem = refs
    tc = COMBINE_TILE
    base = pl.program_id(0) * tc

    def issue(k, carry):
        _row_copy(ys_hbm, buf.at[0], d0_ref[base + k], k, sem).start()
        _row_copy(ys_hbm, buf.at[1], d1_ref[base + k], k, sem).start()
        return carry

    lax.fori_loop(0, tc, issue, 0)
    _rows_wait(ys_hbm, buf.at[0], tc, sem)
    _rows_wait(ys_hbm, buf.at[1], tc, sem)
    slab = slab_ref[...]
    w1, w2 = slab[:, 4:5], slab[:, 5:6]
    y1 = jnp.concatenate([buf[0, :, j, :] for j in range(N_SLAB)], axis=1)
    y2 = jnp.concatenate([buf[1, :, j, :] for j in range(N_SLAB)], axis=1)
    xn = x_ref[...] + mod_ref[0, 5:6, :] * (w1 * y1 + w2 * y2)
    o_ref[...] = _final_norm(xn, fg_ref[...]) if final else xn


def _combine(dest0, dest1, x, mods, slab, ys, rows_per_mod, final_g=None):
    m = x.shape[0]
    tc = COMBINE_TILE
    final = final_g is not None
    mod_map = ((lambda i, a, b: (i // (rows_per_mod // tc), 0, 0)) if rows_per_mod
               else (lambda i, a, b: (0, 0, 0)))
    in_specs = [pl.BlockSpec((tc, D_MODEL), lambda i, a, b: (i, 0)),
                pl.BlockSpec((1, 6, D_MODEL), mod_map),
                pl.BlockSpec((tc, LANE), lambda i, a, b: (i, 0))]
    args = [x, mods, slab]
    if final:
        in_specs.append(pl.BlockSpec((1, D_MODEL), lambda i, a, b: (0, 0)))
        args.append(final_g)
    in_specs.append(pl.BlockSpec(memory_space=pl.ANY))
    args.append(ys)
    return pl.pallas_call(
        functools.partial(_combine_kernel, final=final),
        out_shape=jax.ShapeDtypeStruct((m, D_MODEL), F32),
        grid_spec=pltpu.PrefetchScalarGridSpec(
            num_scalar_prefetch=2,
            grid=(m // tc,),
            in_specs=in_specs,
            out_specs=pl.BlockSpec((tc, D_MODEL), lambda i, a, b: (i, 0)),
            scratch_shapes=[pltpu.VMEM((2, tc, N_SLAB, LANE), F32), pltpu.SemaphoreType.DMA],
        ),
        compiler_params=_cparams(("arbitrary",)),
        name="combine",
    )(dest0, dest1, *args)


def _moe(h_slab, logits, x, mods, wg, wu, wd, rows_per_mod, final_g=None):
    m = x.shape[0]
    t = EXPERT_TILE
    n_rows = TOP_K * m + N_EXP * t
    slab, counts = _route(logits)
    cnt = counts[0, :N_EXP].astype(jnp.int32)
    padded = (cnt + (t - 1)) // t * t
    ends = jnp.cumsum(padded)
    starts = ends - padded
    dest = jnp.take(starts, slab[:, 0:2].astype(jnp.int32)) + slab[:, 2:4].astype(jnp.int32)
    dest0, dest1 = dest[:, 0], dest[:, 1]
    tile_start = jnp.arange(n_rows // t, dtype=jnp.int32) * t
    tile_expert = jnp.minimum(jnp.sum(tile_start[:, None] >= ends[None, :], axis=1), N_EXP - 1).astype(jnp.int32)
    zero_starts = jnp.concatenate([
        jnp.maximum(ends - t, 0),
        jnp.minimum(ends[-1] + jnp.arange(N_EXP, dtype=jnp.int32) * t, n_rows - t)]).astype(jnp.int32)
    xs = _dispatch(dest0, dest1, zero_starts, h_slab, n_rows)
    ys = _experts(tile_expert, xs, wg, wu, wd)
    return _combine(dest0, dest1, x, mods, slab, ys, rows_per_mod, final_g)


def _prep_w_in(w):
    qa, ka, va, ub, qc = (w[:, 0:512], w[:, 512:1024], w[:, 1024:1536], w[:, 1536:2048], w[:, 2048:2560])
    kc, vc = w[:, 2560:2688], w[:, 2688:2816]
    dup = lambda a: jnp.concatenate([a[:, 0:64], a[:, 0:64], a[:, 64:128], a[:, 64:128]], axis=1)
    scale = LOG2E * DHA ** -0.5
    return jnp.concatenate([qa * scale, ka, va, qc * scale, dup(kc), dup(vc), ub], axis=1).astype(BF16)


def _rope_tables(t):
    rows = t // GRID_W
    row = jnp.repeat(jnp.arange(rows, dtype=F32), GRID_W)
    col = jnp.tile(jnp.arange(GRID_W, dtype=F32), rows)
    inv = ROPE_BASE ** (-jnp.arange(0, ROT_HALF, 2, dtype=F32) / ROT_HALF)
    ang_r, ang_c = row[:, None] * inv, col[:, None] * inv
    zero = jnp.zeros_like(ang_r)
    seg = lambda a, b: jnp.concatenate([a, b], axis=1)
    cos = jnp.concatenate([seg(jnp.cos(ang_r), jnp.cos(ang_r)), seg(jnp.cos(ang_c), jnp.cos(ang_c))], axis=1)
    s_next = jnp.concatenate([seg(-jnp.sin(ang_r), zero), seg(-jnp.sin(ang_c), zero)], axis=1)
    s_prev = jnp.concatenate([seg(zero, jnp.sin(ang_r)), seg(zero, jnp.sin(ang_c))], axis=1)
    two = lambda a: jnp.concatenate([a, a], axis=1)
    return two(cos), two(s_next), two(s_prev)


def kernel(x_prompt, x_sample, cache_diff_k, cache_diff_v, cache_win_k, cache_win_v, c, c_ctx, w_ada, b_ada, norm1_g, norm2_g, w_in, w_gate, lam_p, subln_g, w_pool, pool_scale, sink, w_branch, w_out, w_ff_gate, w_ff_up, w_ff_down, w_router, w_ex_gate, w_ex_up, w_ex_down, final_g):
    bp, s, _ = x_prompt.shape
    bs, t, _ = x_sample.shape
    past = cache_diff_k.shape[2]

    cvec = jnp.concatenate([c_ctx[None, :], c, jnp.zeros((MOD_ROWS - 1 - bs, D_MODEL), F32)], axis=0)
    mods = _adaln(cvec, w_ada, b_ada).reshape(DEPTH, MOD_ROWS, 6, D_MODEL)
    rope_tabs = _rope_tables(t)

    ck_a = cache_diff_k.reshape(bs, DEPTH, past, HA * 2 * DHA)
    cv_a = cache_diff_v.reshape(bs, DEPTH, past, HA * 2 * DHA)
    ck_c = cache_win_k.reshape(bs, DEPTH, past, KVC * DHC)
    cv_c = cache_win_v.reshape(bs, DEPTH, past, KVC * DHC)

    xc = x_prompt.reshape(bp * s, D_MODEL)
    xl = x_sample.reshape(bs * t, D_MODEL)
    caches = [[], [], [], []]
    for l in range(DEPTH):
        last = l == DEPTH - 1
        lam_init = 0.8 - 0.6 * math.exp(-0.3 * l)
        w_in_l = _prep_w_in(w_in[l])
        wgate_l = w_gate[l].astype(BF16)
        wpool_l = w_pool[l].astype(BF16)
        wbr_l = w_branch[l].astype(BF16)
        wout_l = w_out[l].astype(BF16)
        g1 = norm1_g[l][None, :]
        g2 = norm2_g[l][None, :]
        sg = subln_g[l][None, :]
        ps = pool_scale[l][None, :]
        fg = final_g[None, :] if last else None
        i = l // 2
        if l % 2 == 0:
            wr = None
            wg, wu, wd = (w_ff_gate[i].astype(BF16), w_ff_up[i].astype(BF16), w_ff_down[i].astype(BF16))
        else:
            wr_f = jnp.pad(w_router[i], ((0, 0), (0, LANE - N_EXP)))
            wr_hi = wr_f.astype(BF16)
            wr = (wr_hi, (wr_f - wr_hi.astype(F32)).astype(BF16))
            wg, wu, wd = (w_ex_gate[i].astype(BF16), w_ex_up[i].astype(BF16), w_ex_down[i].astype(BF16))

        def channel_mixer(res, mod, rows_per_mod):
            if wr is None:
                assert fg is None
                return _ffn(res[1], res[0], mod, wg, wu, wd, tm=512, rows_per_mod=rows_per_mod)
            return _moe(res[1], res[2], res[0], mod, wg, wu, wd, rows_per_mod, final_g=fg)

        m_c = mods[l, 0:1]
        pc, ubc, dk, dv, wk, wv = _inproj(xc, g1, m_c, w_in_l, tm=256, rows_per_mod=0)
        for lst, a in zip(caches, (dk, dv, wk, wv)):
            lst.append(a)
        oa = _diff_ctx(pc, lam_p[l], sg, lam_init, bp, s)
        pooled = _pool(ubc, bp, s)
        oc = _gqa_ctx(pc, sink[l], bp, s)
        res = _merge(xc, oa, pooled, oc, g1, g2, m_c, wgate_l, wpool_l, ps, wbr_l, wout_l,
                     tm=256, rows_per_mod=0, wr=wr)
        xc = channel_mixer(res, m_c, 0)

        m_l = mods[l, 1:1 + bs]
        pl_, ubl = _inproj(xl, g1, m_l, w_in_l, tm=256, rows_per_mod=t, rope_tabs=rope_tabs)
        oa = _diff_lat(pl_, ck_a, cv_a, l, lam_p[l], sg, lam_init, bs, t, tq=128)
        pooled = _pool(ubl, bs, t)
        oc = _gqa_lat(pl_, ck_c, cv_c, l, sink[l], bs, t)
        res = _merge(xl, oa, pooled, oc, g1, g2, m_l, wgate_l, wpool_l, ps, wbr_l, wout_l,
                     tm=256, rows_per_mod=t, wr=wr)
        xl = channel_mixer(res, m_l, t)

    y_prompt = xc.reshape(bp, s, D_MODEL)
    y_sample = xl.reshape(bs, t, D_MODEL)
    new_diff_k = jnp.stack(caches[0], axis=0).reshape(DEPTH, bp, s, HA, 2, DHA).swapaxes(0, 1)
    new_diff_v = jnp.stack(caches[1], axis=0).reshape(DEPTH, bp, s, HA, 2 * DHA).swapaxes(0, 1)
    new_win_k = jnp.stack(caches[2], axis=0).reshape(DEPTH, bp, s, KVC, DHC).swapaxes(0, 1)
    new_win_v = jnp.stack(caches[3], axis=0).reshape(DEPTH, bp, s, KVC, DHC).swapaxes(0, 1)
    return (y_prompt, y_sample, new_diff_k, new_diff_v, new_win_k, new_win_v)
```

```python
import functools
import math

import numpy as np
import jax
import jax.numpy as jnp
from jax import lax
from jax.experimental import pallas as pl
from jax.experimental.pallas import tpu as pltpu

F32 = jnp.float32
BF16 = jnp.bfloat16

D_MODEL = 1024
DEPTH = 2
GRID_W = 64
HA = 4
DHA = 64
HC = 8
KVC = 2
REPC = HC // KVC
DHC = 64
WINDOW = 128
QBLK = 128
POOL_WINDOWS = (2, 4, 8, 16)
POOL_GC = 128
POOL_W = 512
BRANCH_W = 512
N_BRANCH = 3
ROPE_BASE = 10000.0
ROT_HALF = 32
N_EXP = 8
TOP_K = 2
D_FF_E = 1408
EPS = 1e-6
NEG_INF = -1e30
LOG2E = math.log2(math.e)

LANE = 128
MOD_ROWS = 16
P_W = 20 * LANE
IN_W2 = 24 * LANE
VMEM_LIMIT = 56 * 1024 * 1024


def _cparams(sem):
    return pltpu.CompilerParams(dimension_semantics=sem, vmem_limit_bytes=VMEM_LIMIT)


def _dot(a, b):
    return jnp.dot(a, b, preferred_element_type=F32)


def _dot_nt(a, b):
    return lax.dot_general(a, b, (((1,), (1,)), ((), ())), preferred_element_type=F32)


def _sigmoid(x):
    return 1.0 / (1.0 + jnp.exp(-x))


def _modnorm(x, g, shift, scale):
    ms = jnp.mean(x * x, axis=-1, keepdims=True)
    return (x * lax.rsqrt(ms + EPS) * g) * (1.0 + scale) + shift


def _adaln_kernel(c_ref, w_ref, b_ref, o_ref):
    c = c_ref[...]
    s = c * _sigmoid(c)
    o_ref[0] = _dot(s.astype(BF16), w_ref[0].astype(BF16)) + b_ref[0]


def _adaln(cvec, w_ada, b_ada):
    tn = 1536
    n = 6 * D_MODEL
    return pl.pallas_call(
        _adaln_kernel,
        out_shape=jax.ShapeDtypeStruct((DEPTH, MOD_ROWS, n), F32),
        grid=(DEPTH, n // tn),
        in_specs=[
            pl.BlockSpec((MOD_ROWS, D_MODEL), lambda l, j: (0, 0)),
            pl.BlockSpec((1, D_MODEL, tn), lambda l, j: (l, 0, j)),
            pl.BlockSpec((1, 1, tn), lambda l, j: (l, 0, j)),
        ],
        out_specs=pl.BlockSpec((1, MOD_ROWS, tn), lambda l, j: (l, 0, j)),
        compiler_params=_cparams(("arbitrary", "arbitrary")),
        name="adaln",
    )(cvec, w_ada, b_ada.reshape(DEPTH, 1, n))


def _rope(p, cos, sin_next, sin_prev):
    nxt = pltpu.roll(p, LANE - 16, 1)
    prv = pltpu.roll(p, 16, 1)
    return p * cos + nxt * sin_next + prv * sin_prev


def _inproj_kernel(*refs, rope):
    if rope:
        (x_ref, g_ref, mod_ref, w_ref, cos_ref, sn_ref, sp_ref, p_ref, ub_ref) = refs
    else:
        (x_ref, g_ref, mod_ref, w_ref, p_ref, ub_ref, dk_ref, dv_ref, wk_ref, wv_ref) = refs
    h = _modnorm(x_ref[...], g_ref[...], mod_ref[0, 0:1, :], mod_ref[0, 1:2, :])
    hb = h.astype(BF16)
    if rope:
        cos, sn, sp = cos_ref[...], sn_ref[...], sp_ref[...]
    lane = lax.broadcasted_iota(jnp.int32, (1, LANE), 1)
    for c in range(IN_W2 // 512):
        pc = _dot(hb, w_ref[:, c * 512:(c + 1) * 512])
        blocks = [pc[:, j * LANE:(j + 1) * LANE] for j in range(4)]
        if c == 5:
            ub_ref[...] = pc
            continue
        if rope:
            n_rot = {0: 4, 1: 4, 2: 0, 3: 4, 4: 2}[c]
            blocks = [_rope(b, cos, sn, sp) if j < n_rot else b for j, b in enumerate(blocks)]
        elif c == 1:
            dk_ref[...] = pc
        elif c == 2:
            dv_ref[...] = pc
        elif c == 4:
            wk_ref[...] = jnp.where(lane < DHC, blocks[0], blocks[1])
            wv_ref[...] = jnp.where(lane < DHC, blocks[2], blocks[3])
        for j in range(4):
            p_ref[:, (4 * c + j) * LANE:(4 * c + j + 1) * LANE] = blocks[j].astype(BF16)


def _inproj(x, g, mods, w, tm, rows_per_mod, rope_tabs=None):
    m = x.shape[0]
    rope = rope_tabs is not None
    mod_map = (lambda i: (i // (rows_per_mod // tm), 0, 0)) if rows_per_mod else (lambda i: (0, 0, 0))
    in_specs = [
        pl.BlockSpec((tm, D_MODEL), lambda i: (i, 0)),
        pl.BlockSpec((1, D_MODEL), lambda i: (0, 0)),
        pl.BlockSpec((1, 6, D_MODEL), mod_map),
        pl.BlockSpec((D_MODEL, IN_W2), lambda i: (0, 0)),
    ]
    args = [x, g, mods, w]
    out_shape = [jax.ShapeDtypeStruct((m, P_W), BF16), jax.ShapeDtypeStruct((m, POOL_W), F32)]
    out_specs = [pl.BlockSpec((tm, P_W), lambda i: (i, 0)), pl.BlockSpec((tm, POOL_W), lambda i: (i, 0))]
    if rope:
        t = rope_tabs[0].shape[0]
        for tab in rope_tabs:
            in_specs.append(pl.BlockSpec((tm, LANE), lambda i: (i % (t // tm), 0)))
            args.append(tab)
    else:
        for wdt in (512, 512, LANE, LANE):
            out_shape.append(jax.ShapeDtypeStruct((m, wdt), F32))
            out_specs.append(pl.BlockSpec((tm, wdt), lambda i: (i, 0)))
    return pl.pallas_call(
        functools.partial(_inproj_kernel, rope=rope),
        out_shape=out_shape,
        grid=(m // tm,),
        in_specs=in_specs,
        out_specs=out_specs,
        compiler_params=_cparams(("arbitrary",)),
        name="inproj_rope" if rope else "inproj",
    )(*args)


def _diff_lambda(lam_ref, lam_init):
    lf = lam_ref[...]
    s1 = jnp.sum(lf[0:1] * lf[1:2], axis=-1, keepdims=True)
    s2 = jnp.sum(lf[2:3] * lf[3:4], axis=-1, keepdims=True)
    return jnp.exp(s1) - jnp.exp(s2) + lam_init


def _diff_scores(q, k):
    lane = lax.broadcasted_iota(jnp.int32, (1, LANE), 1)
    zero = jnp.zeros_like(q)
    q2 = jnp.concatenate([jnp.where(lane < DHA, q, zero), jnp.where(lane < DHA, zero, q)], axis=0)
    return _dot_nt(q2, k)


def _diff_finish(s, v, lam, g, lam_init):
    tq = s.shape[0] // 2
    e = jnp.exp2(s - jnp.max(s, axis=-1, keepdims=True))
    inv = 1.0 / jnp.sum(e, axis=-1, keepdims=True)
    o2 = _dot(e.astype(BF16), v) * inv
    o = o2[:tq] - lam * o2[tq:]
    ms = jnp.mean(o * o, axis=-1, keepdims=True)
    return ((o * lax.rsqrt(ms + EPS) * g) * (1.0 - lam_init)).astype(BF16)


def _diff_tile(q, k, v, lam, g, lam_init):
    return _diff_finish(_diff_scores(q, k), v, lam, g, lam_init)


def _diff_ctx_kernel(q_ref, k_ref, v_ref, lam_ref, g_ref, o_ref, *, lam_init):
    lam = _diff_lambda(lam_ref, lam_init)
    g = g_ref[...]
    for h in range(HA):
        sl = slice(h * LANE, (h + 1) * LANE)
        o_ref[:, sl] = _diff_tile(q_ref[:, sl], k_ref[:, sl], v_ref[:, sl], lam, g, lam_init)


def _diff_ctx(p, lam_p, subln_g, lam_init, nb, s):
    w = HA * LANE
    return pl.pallas_call(
        functools.partial(_diff_ctx_kernel, lam_init=lam_init),
        out_shape=jax.ShapeDtypeStruct((nb * s, w), BF16),
        grid=(nb,),
        in_specs=[
            pl.BlockSpec((s, w), lambda b: (b, 0)),
            pl.BlockSpec((s, w), lambda b: (b, 1)),
            pl.BlockSpec((s, w), lambda b: (b, 2)),
            pl.BlockSpec((4, DHA), lambda b: (0, 0)),
            pl.BlockSpec((1, LANE), lambda b: (0, 0)),
        ],
        out_specs=pl.BlockSpec((s, w), lambda b: (b, 0)),
        compiler_params=_cparams(("arbitrary",)),
        name="diff_ctx",
    )(p, p, p, lam_p, subln_g)


def _diff_lat_kernel(q_ref, kn_ref, vn_ref, kc_ref, vc_ref, lam_ref, g_ref, o_ref,
                     k_all, v_all, s_even, s_odd, *, lam_init, past, tq):
    k_all[0:past, :] = kc_ref[0, 0].astype(BF16)
    v_all[0:past, :] = vc_ref[0, 0].astype(BF16)
    k_all[past:, :] = kn_ref[...]
    v_all[past:, :] = vn_ref[...]
    lam = _diff_lambda(lam_ref, lam_init)
    g = g_ref[...]
    n_tiles = q_ref.shape[0] // tq

    def rows(i):
        return pl.ds(pl.multiple_of(i * tq, tq), tq)

    s_even[...] = _diff_scores(q_ref[rows(0), :], k_all[...])

    def body(j, carry):
        i = 2 * j
        s_odd[...] = _diff_scores(q_ref[rows(i + 1), :], k_all[...])
        o_ref[rows(i), :] = _diff_finish(s_even[...], v_all[...], lam, g, lam_init)
        s_even[...] = _diff_scores(q_ref[rows(jnp.minimum(i + 2, n_tiles - 1)), :], k_all[...])
        o_ref[rows(i + 1), :] = _diff_finish(s_odd[...], v_all[...], lam, g, lam_init)
        return carry

    lax.fori_loop(0, n_tiles // 2, body, 0)


def _diff_lat(p, cache_k, cache_v, l, lam_p, subln_g, lam_init, nb, t, tq):
    past = cache_k.shape[2]
    return pl.pallas_call(
        functools.partial(_diff_lat_kernel, lam_init=lam_init, past=past, tq=tq),
        out_shape=jax.ShapeDtypeStruct((nb * t, HA * LANE), BF16),
        grid=(nb, HA),
        in_specs=[
            pl.BlockSpec((t, LANE), lambda b, h: (b, h)),
            pl.BlockSpec((t, LANE), lambda b, h: (b, 4 + h)),
            pl.BlockSpec((t, LANE), lambda b, h: (b, 8 + h)),
            pl.BlockSpec((1, 1, past, LANE), lambda b, h: (b, l, 0, h)),
            pl.BlockSpec((1, 1, past, LANE), lambda b, h: (b, l, 0, h)),
            pl.BlockSpec((4, DHA), lambda b, h: (0, 0)),
            pl.BlockSpec((1, LANE), lambda b, h: (0, 0)),
        ],
        out_specs=pl.BlockSpec((t, LANE), lambda b, h: (b, h)),
        scratch_shapes=[pltpu.VMEM((past + t, LANE), BF16), pltpu.VMEM((past + t, LANE), BF16),
                        pltpu.VMEM((2 * tq, past + t), F32), pltpu.VMEM((2 * tq, past + t), F32)],
        compiler_params=_cparams(("arbitrary", "arbitrary")),
        name="diff_lat",
    )(p, p, p, cache_k, cache_v, lam_p, subln_g)


POOL_PAD = 16


def _pool_kernel(u_ref, o_ref, pad_ref, *, t):
    tp = t + POOL_PAD
    pos = lax.broadcasted_iota(jnp.int32, (t, 1), 0)
    for gi, w in enumerate(POOL_WINDOWS):
        sl = slice(gi * POOL_GC, (gi + 1) * POOL_GC)
        u = u_ref[0, :, sl]
        pad_ref[0:t, :] = u
        pad_ref[t:tp, :] = jnp.zeros((POOL_PAD, POOL_GC), F32)
        acc = pad_ref[...]
        step = 1
        while step < w:
            acc = acc + pltpu.roll(acc, step, 0)
            step *= 2
        ahead = w - w // 2 - 1
        if ahead:
            acc = pltpu.roll(acc, tp - ahead, 0)
        lo = jnp.maximum(pos - w // 2, 0)
        hi = jnp.minimum(pos + (w - w // 2), t)
        mean = acc[0:t] / (hi - lo).astype(F32)
        o_ref[0, :, sl] = (mean - u).astype(BF16)


def _pool(ub, nb, t):
    return pl.pallas_call(
        functools.partial(_pool_kernel, t=t),
        out_shape=jax.ShapeDtypeStruct((nb, t, POOL_W), BF16),
        grid=(nb,),
        in_specs=[pl.BlockSpec((1, t, POOL_W), lambda b: (b, 0, 0))],
        out_specs=pl.BlockSpec((1, t, POOL_W), lambda b: (b, 0, 0)),
        scratch_shapes=[pltpu.VMEM((t + POOL_PAD, POOL_GC), F32)],
        compiler_params=_cparams(("arbitrary",)),
        name="pool",
    )(ub.reshape(nb, t, POOL_W)).reshape(nb * t, POOL_W)


def _stack_heads(q):
    lane = lax.broadcasted_iota(jnp.int32, (1, LANE), 1)
    zero = jnp.zeros_like(q[:, :LANE])
    parts = []
    for pair in range(2):
        qp = q[:, pair * LANE:(pair + 1) * LANE]
        parts.append(jnp.where(lane < DHC, qp, zero))
        parts.append(jnp.where(lane < DHC, zero, qp))
    return jnp.concatenate(parts, axis=0)


def _gqa_core(score_parts, masks, value_parts, sink_ref, g, tq):
    lane = lax.broadcasted_iota(jnp.int32, (1, LANE), 1)
    e_parts = [[] for _ in score_parts]
    invs = []
    for r in range(REPC):
        rows = slice(r * tq, (r + 1) * tq)
        sink = sink_ref[g * REPC + r] * LOG2E
        parts = [s[rows] if mk is None else jnp.where(mk, s[rows], NEG_INF)
                 for s, mk in zip(score_parts, masks)]
        m = jnp.maximum(functools.reduce(jnp.maximum, [jnp.max(s, axis=-1, keepdims=True) for s in parts]), sink)
        es = [jnp.exp2(s - m) for s in parts]
        den = functools.reduce(lambda a, b: a + b, [jnp.sum(e, axis=-1, keepdims=True) for e in es])
        invs.append(1.0 / (den + jnp.exp2(sink - m)))
        for lst, e in zip(e_parts, es):
            lst.append(e.astype(BF16))
    o = functools.reduce(lambda a, b: a + b,
                         [_dot(jnp.concatenate(lst, axis=0), v) for lst, v in zip(e_parts, value_parts)])
    o = o * jnp.concatenate(invs, axis=0)
    outs = [o[r * tq:(r + 1) * tq] for r in range(REPC)]
    return jnp.concatenate([jnp.where(lane < DHC, outs[0], outs[1]),
                            jnp.where(lane < DHC, outs[2], outs[3])], axis=1).astype(BF16)


def _gqa_ctx_kernel(sink_ref, q_ref, k_ref, v_ref, o_ref, *, s):
    lane = lax.broadcasted_iota(jnp.int32, (1, LANE), 1)
    for g in range(KVC):
        q4 = _stack_heads(q_ref[:, g * 2 * LANE:(g + 1) * 2 * LANE])
        sc = _dot_nt(q4, k_ref[:, g * LANE:(g + 1) * LANE])
        v = v_ref[:, g * LANE:(g + 1) * LANE]
        outs = []
        for r in range(REPC):
            sr = sc[r * s:(r + 1) * s]
            sink = sink_ref[g * REPC + r] * LOG2E
            m = jnp.maximum(jnp.max(sr, axis=-1, keepdims=True), sink)
            e = jnp.exp2(sr - m)
            inv = 1.0 / (jnp.sum(e, axis=-1, keepdims=True) + jnp.exp2(sink - m))
            outs.append(_dot(e.astype(BF16), v) * inv)
        o_ref[:, g * 2 * LANE:(g + 1) * 2 * LANE] = jnp.concatenate(
            [jnp.where(lane < DHC, outs[0], outs[1]), jnp.where(lane < DHC, outs[2], outs[3])], axis=1).astype(BF16)


def _gqa_ctx(p, sink, nb, s):
    w = HC * DHC
    return pl.pallas_call(
        functools.partial(_gqa_ctx_kernel, s=s),
        out_shape=jax.ShapeDtypeStruct((nb * s, w), BF16),
        grid=(nb,),
        in_specs=[
            pl.BlockSpec(memory_space=pltpu.SMEM),
            pl.BlockSpec((s, w), lambda b: (b, 3)),
            pl.BlockSpec((s, KVC * LANE), lambda b: (b, 8)),
            pl.BlockSpec((s, KVC * LANE), lambda b: (b, 9)),
        ],
        out_specs=pl.BlockSpec((s, w), lambda b: (b, 0)),
        compiler_params=_cparams(("arbitrary",)),
        name="gqa_ctx",
    )(sink, p, p, p)


def _gqa_lat_kernel(sink_ref, q_ref, kn_ref, vn_ref, kc_ref, vc_ref, o_ref, kc_dup, vc_dup,
                    sc_even, sw_even, sc_odd, sw_odd, *, t):
    g = pl.program_id(1)
    lane = lax.broadcasted_iota(jnp.int32, (1, LANE), 1)
    keep = jnp.where(lane < DHC, 0, 1) == g
    for src, dst in ((kc_ref, kc_dup), (vc_ref, vc_dup)):
        blk = src[0, 0]
        dst[...] = jnp.where(keep, blk, pltpu.roll(blk, DHC, 1)).astype(BF16)
    span = QBLK + 2 * WINDOW
    n_blocks = t // QBLK

    def rows(i):
        return pl.ds(pl.multiple_of(i * QBLK, QBLK), QBLK)

    def win_start(i):
        return pl.multiple_of(jnp.clip((i - 1) * QBLK, 0, t - span), QBLK)

    def scores(i, sc_ref, sw_ref):
        q4 = _stack_heads(q_ref[rows(i), :])
        sc_ref[...] = _dot_nt(q4, kc_dup[...])
        sw_ref[...] = _dot_nt(q4, kn_ref[pl.ds(win_start(i), span), :])

    def finish(i, sc_ref, sw_ref):
        w0 = win_start(i)
        qpos = i * QBLK + lax.broadcasted_iota(jnp.int32, (QBLK, span), 0)
        kpos = w0 + lax.broadcasted_iota(jnp.int32, (QBLK, span), 1)
        band = jnp.abs(qpos - kpos) <= WINDOW
        o_ref[rows(i), :] = _gqa_core([sc_ref[...], sw_ref[...]], [None, band],
                                      [vc_dup[...], vn_ref[pl.ds(w0, span), :]], sink_ref, g, QBLK)

    scores(0, sc_even, sw_even)

    def body(j, carry):
        i = 2 * j
        scores(i + 1, sc_odd, sw_odd)
        finish(i, sc_even, sw_even)
        scores(jnp.minimum(i + 2, n_blocks - 1), sc_even, sw_even)
        finish(i + 1, sc_odd, sw_odd)
        return carry

    lax.fori_loop(0, n_blocks // 2, body, 0)


def _gqa_lat(p, cache_k, cache_v, l, sink, nb, t):
    past = cache_k.shape[2]
    return pl.pallas_call(
        functools.partial(_gqa_lat_kernel, t=t),
        out_shape=jax.ShapeDtypeStruct((nb * t, HC * DHC), BF16),
        grid=(nb, KVC),
        in_specs=[
            pl.BlockSpec(memory_space=pltpu.SMEM),
            pl.BlockSpec((t, 2 * LANE), lambda b, g: (b, 6 + g)),
            pl.BlockSpec((t, LANE), lambda b, g: (b, 16 + g)),
            pl.BlockSpec((t, LANE), lambda b, g: (b, 18 + g)),
            pl.BlockSpec((1, 1, past, LANE), lambda b, g: (b, l, 0, 0)),
            pl.BlockSpec((1, 1, past, LANE), lambda b, g: (b, l, 0, 0)),
        ],
        out_specs=pl.BlockSpec((t, 2 * LANE), lambda b, g: (b, g)),
        scratch_shapes=[pltpu.VMEM((past, LANE), BF16), pltpu.VMEM((past, LANE), BF16)]
        + [pltpu.VMEM((REPC * QBLK, past), F32), pltpu.VMEM((REPC * QBLK, QBLK + 2 * WINDOW), F32)] * 2,
        compiler_params=_cparams(("arbitrary", "arbitrary")),
        name="gqa_lat",
    )(sink, p, p, p, cache_k, cache_v)


def _merge_kernel(*refs, router):
    (x_ref, oa_ref, pb_ref, oc_ref, g1_ref, g2_ref, mod_ref, wgate_ref, wpool_ref, ps_ref,
     wbr_ref, wout_ref) = refs[:12]
    if router:
        wr_hi_ref, wr_lo_ref, xo_ref, h2_ref, lg_ref = refs[12:]
    else:
        xo_ref, h2_ref = refs[12:]
    x = x_ref[...]
    h = _modnorm(x, g1_ref[...], mod_ref[0, 0:1, :], mod_ref[0, 1:2, :])
    hb = h.astype(BF16)
    pooled = pb_ref[...]
    yb = jnp.concatenate(
        [_dot(pooled[:, j * POOL_GC:(j + 1) * POOL_GC], wpool_ref[j]) for j in range(len(POOL_WINDOWS))],
        axis=1) * ps_ref[...]
    branches = (oa_ref[...], yb.astype(BF16), oc_ref[...])
    acc = None
    for n in range(N_BRANCH):
        gate = _sigmoid(_dot(hb, wgate_ref[:, n * D_MODEL:(n + 1) * D_MODEL]))
        term = gate * _dot(branches[n], wbr_ref[n])
        acc = term if acc is None else acc + term
    xn = x + mod_ref[0, 2:3, :] * _dot(acc.astype(BF16), wout_ref[...])
    xo_ref[...] = xn
    h2 = _modnorm(xn, g2_ref[...], mod_ref[0, 3:4, :], mod_ref[0, 4:5, :])
    h2_hi = h2.astype(BF16)
    if router:
        for j in range(D_MODEL // LANE):
            h2_ref[:, j, :] = h2[:, j * LANE:(j + 1) * LANE]
        h2_lo = (h2 - h2_hi.astype(F32)).astype(BF16)
        w_hi, w_lo = wr_hi_ref[...], wr_lo_ref[...]
        lg_ref[...] = (_dot(h2_hi, w_hi) + (_dot(h2_hi, w_lo) + _dot(h2_lo, w_hi))) + _dot(h2_lo, w_lo)
    else:
        h2_ref[...] = h2_hi


def _merge(x, oa, pooled, oc, g1, g2, mods, wgate, wpool, pscale, wbr, wout, tm, rows_per_mod, wr=None):
    m = x.shape[0]
    router = wr is not None
    mod_map = (lambda i: (i // (rows_per_mod // tm), 0, 0)) if rows_per_mod else (lambda i: (0, 0, 0))
    row = lambda w: pl.BlockSpec((tm, w), lambda i: (i, 0))
    full = lambda a: pl.BlockSpec(a.shape, lambda i: (0,) * a.ndim)
    in_specs = [row(D_MODEL), row(BRANCH_W), row(BRANCH_W), row(BRANCH_W), full(g1), full(g2),
                pl.BlockSpec((1, 6, D_MODEL), mod_map), full(wgate), full(wpool), full(pscale),
                full(wbr), full(wout)]
    args = [x, oa, pooled, oc, g1, g2, mods, wgate, wpool, pscale, wbr, wout]
    out_shape = [jax.ShapeDtypeStruct((m, D_MODEL), F32)]
    out_specs = [row(D_MODEL)]
    if router:
        in_specs += [full(wr[0]), full(wr[1])]
        args += [wr[0], wr[1]]
        out_shape += [jax.ShapeDtypeStruct((m, D_MODEL // LANE, LANE), F32), jax.ShapeDtypeStruct((m, LANE), F32)]
        out_specs += [pl.BlockSpec((tm, D_MODEL // LANE, LANE), lambda i: (i, 0, 0)), row(LANE)]
    else:
        out_shape.append(jax.ShapeDtypeStruct((m, D_MODEL), BF16))
        out_specs.append(row(D_MODEL))
    return pl.pallas_call(
        functools.partial(_merge_kernel, router=router),
        out_shape=out_shape,
        grid=(m // tm,),
        in_specs=in_specs,
        out_specs=out_specs,
        compiler_params=_cparams(("arbitrary",)),
        name="merge",
    )(*args)


def _final_norm(xn, fg):
    ms = jnp.mean(xn * xn, axis=-1, keepdims=True)
    return xn * lax.rsqrt(ms + EPS) * fg


def _swiglu(h, wg, wu, wd):
    a = _dot(h, wg)
    u = _dot(h, wu)
    return _dot(((a * _sigmoid(a)) * u).astype(BF16), wd)


def _ffn_kernel(h_ref, x_ref, mod_ref, wg_ref, wu_ref, wd_ref, o_ref, acc_ref):
    e = pl.program_id(1)

    @pl.when(e == 0)
    def _():
        acc_ref[...] = jnp.zeros_like(acc_ref)

    acc_ref[...] += _swiglu(h_ref[...], wg_ref[...], wu_ref[...], wd_ref[...])

    @pl.when(e == pl.num_programs(1) - 1)
    def _():
        o_ref[...] = x_ref[...] + mod_ref[0, 5:6, :] * acc_ref[...]


def _ffn(h2, x, mods, wg, wu, wd, tm, rows_per_mod):
    m = x.shape[0]
    mod_map = (lambda i, e: (i // (rows_per_mod // tm), 0, 0)) if rows_per_mod else (lambda i, e: (0, 0, 0))
    row = lambda w: pl.BlockSpec((tm, w), lambda i, e: (i, 0))
    return pl.pallas_call(
        _ffn_kernel,
        out_shape=jax.ShapeDtypeStruct((m, D_MODEL), F32),
        grid=(m // tm, wg.shape[1] // D_FF_E),
        in_specs=[row(D_MODEL), row(D_MODEL), pl.BlockSpec((1, 6, D_MODEL), mod_map),
                  pl.BlockSpec((D_MODEL, D_FF_E), lambda i, e: (0, e)),
                  pl.BlockSpec((D_MODEL, D_FF_E), lambda i, e: (0, e)),
                  pl.BlockSpec((D_FF_E, D_MODEL), lambda i, e: (e, 0))],
        out_specs=row(D_MODEL),
        scratch_shapes=[pltpu.VMEM((tm, D_MODEL), F32)],
        compiler_params=_cparams(("arbitrary", "arbitrary")),
        name="ffn",
    )(h2, x, mods, wg, wu, wd)


ROUTE_TILE = 512
EXPERT_TILE = 512
DISPATCH_CHUNK = 512
COMBINE_TILE = 256
N_SLAB = D_MODEL // LANE


def _route_kernel(lg_ref, slab_ref, cnt_ref, carry_ref):
    @pl.when(pl.program_id(0) == 0)
    def _():
        carry_ref[...] = jnp.zeros_like(carry_ref)

    logits = lg_ref[...]
    tr = logits.shape[0]
    lane = lax.broadcasted_iota(jnp.int32, logits.shape, 1).astype(F32)
    neg = -jnp.inf
    l1 = jnp.where(lane < N_EXP, logits, neg)
    m1 = jnp.max(l1, axis=-1, keepdims=True)
    i1 = jnp.min(jnp.where(l1 == m1, lane, float(LANE)), axis=-1, keepdims=True)
    l2 = jnp.where(lane == i1, neg, l1)
    m2 = jnp.max(l2, axis=-1, keepdims=True)
    i2 = jnp.min(jnp.where(l2 == m2, lane, float(LANE)), axis=-1, keepdims=True)
    e2 = jnp.exp(m2 - m1)
    den = 1.0 + e2
    oh1 = jnp.where(lane == i1, 1.0, 0.0)
    oh2 = jnp.where(lane == i2, 1.0, 0.0)
    oh = oh1 + oh2
    r = lax.broadcasted_iota(jnp.int32, (tr, tr), 0)
    c = lax.broadcasted_iota(jnp.int32, (tr, tr), 1)
    before = jnp.where(c < r, 1.0, 0.0).astype(BF16)
    seen = _dot(before, oh.astype(BF16)) + carry_ref[...]
    r1 = jnp.sum(seen * oh1, axis=-1, keepdims=True)
    r2 = jnp.sum(seen * oh2, axis=-1, keepdims=True)
    carry_ref[...] += jnp.sum(oh, axis=0, keepdims=True)
    slab = jnp.zeros_like(logits)
    for j, v in enumerate((i1, i2, r1, r2, 1.0 / den, e2 / den)):
        slab = jnp.where(lane == float(j), v, slab)
    slab_ref[...] = slab
    cnt_ref[...] = jnp.broadcast_to(carry_ref[...], cnt_ref.shape)


def _route(logits):
    m = logits.shape[0]
    return pl.pallas_call(
        _route_kernel,
        out_shape=[jax.ShapeDtypeStruct((m, LANE), F32), jax.ShapeDtypeStruct((8, LANE), F32)],
        grid=(m // ROUTE_TILE,),
        in_specs=[pl.BlockSpec((ROUTE_TILE, LANE), lambda i: (i, 0))],
        out_specs=[pl.BlockSpec((ROUTE_TILE, LANE), lambda i: (i, 0)), pl.BlockSpec((8, LANE), lambda i: (0, 0))],
        scratch_shapes=[pltpu.VMEM((1, LANE), F32)],
        compiler_params=_cparams(("arbitrary",)),
        name="route",
    )(logits)


def _row_copy(src, dst, src_row, dst_row, sem):
    return pltpu.make_async_copy(src.at[src_row], dst.at[dst_row], sem)


def _rows_wait(src, dst, n, sem):
    pltpu.make_async_copy(src.at[pl.ds(0, n)], dst.at[pl.ds(0, n)], sem).wait()


def _dispatch_kernel(d0_ref, d1_ref, zs_ref, h_ref, xs_hbm, zbuf, sem, zsem):
    tc = DISPATCH_CHUNK
    base = pl.program_id(0) * tc

    @pl.when(pl.program_id(0) == 0)
    def _():
        zbuf[...] = jnp.zeros_like(zbuf)
        for j in range(2 * N_EXP):
            cp = pltpu.make_async_copy(zbuf, xs_hbm.at[pl.ds(zs_ref[j], EXPERT_TILE)], zsem)
            cp.start()
            cp.wait()

    def row(k, carry):
        _row_copy(h_ref, xs_hbm, k, d0_ref[base + k], sem).start()
        _row_copy(h_ref, xs_hbm, k, d1_ref[base + k], sem).start()
        return carry

    lax.fori_loop(0, tc, row, 0)
    _rows_wait(h_ref, xs_hbm, tc, sem)
    _rows_wait(h_ref, xs_hbm, tc, sem)


def _dispatch(dest0, dest1, zero_starts, h_slab, n_rows):
    m = h_slab.shape[0]
    tc = DISPATCH_CHUNK
    return pl.pallas_call(
        _dispatch_kernel,
        out_shape=jax.ShapeDtypeStruct((n_rows, N_SLAB, LANE), F32),
        grid_spec=pltpu.PrefetchScalarGridSpec(
            num_scalar_prefetch=3,
            grid=(m // tc,),
            in_specs=[pl.BlockSpec((tc, N_SLAB, LANE), lambda i, a, b, c: (i, 0, 0))],
            out_specs=pl.BlockSpec(memory_space=pl.ANY),
            scratch_shapes=[pltpu.VMEM((EXPERT_TILE, N_SLAB, LANE), F32),
                            pltpu.SemaphoreType.DMA, pltpu.SemaphoreType.DMA],
        ),
        compiler_params=_cparams(("arbitrary",)),
        name="dispatch",
    )(dest0, dest1, zero_starts, h_slab)


def _experts_kernel(te_ref, x_ref, wg_ref, wu_ref, wd_ref, y_ref):
    x = jnp.concatenate([x_ref[:, j, :] for j in range(N_SLAB)], axis=1).astype(BF16)
    y = _swiglu(x, wg_ref[...], wu_ref[...], wd_ref[...])
    for j in range(N_SLAB):
        y_ref[:, j, :] = y[:, j * LANE:(j + 1) * LANE]


def _experts(tile_expert, xs, wg, wu, wd):
    n_rows = xs.shape[0]
    slab = pl.BlockSpec((EXPERT_TILE, N_SLAB, LANE), lambda i, te: (i, 0, 0))
    return pl.pallas_call(
        _experts_kernel,
        out_shape=jax.ShapeDtypeStruct((n_rows, N_SLAB, LANE), F32),
        grid_spec=pltpu.PrefetchScalarGridSpec(
            num_scalar_prefetch=1,
            grid=(n_rows // EXPERT_TILE,),
            in_specs=[slab,
                      pl.BlockSpec((None, D_MODEL, D_FF_E), lambda i, te: (te[i], 0, 0)),
                      pl.BlockSpec((None, D_MODEL, D_FF_E), lambda i, te: (te[i], 0, 0)),
                      pl.BlockSpec((None, D_FF_E, D_MODEL), lambda i, te: (te[i], 0, 0))],
            out_specs=slab,
        ),
        compiler_params=_cparams(("arbitrary",)),
        name="experts",
    )(tile_expert, xs, wg, wu, wd)


def _combine_kernel(*refs, final):
    if final:
        d0_ref, d1_ref, x_ref, mod_ref, slab_ref, fg_ref, ys_hbm, o_ref, buf, sem = refs
    else:
        d0_ref, d1_ref, x_ref, mod_ref, slab_ref, ys_hbm, o_ref, buf, sem = refs
    tc = COMBINE_TILE
    base = pl.program_id(0) * tc

    def issue(k, carry):
        _row_copy(ys_hbm, buf.at[0], d0_ref[base + k], k, sem).start()
        _row_copy(ys_hbm, buf.at[1], d1_ref[base + k], k, sem).start()
        return carry

    lax.fori_loop(0, tc, issue, 0)
    _rows_wait(ys_hbm, buf.at[0], tc, sem)
    _rows_wait(ys_hbm, buf.at[1], tc, sem)
    slab = slab_ref[...]
    w1, w2 = slab[:, 4:5], slab[:, 5:6]
    y1 = jnp.concatenate([buf[0, :, j, :] for j in range(N_SLAB)], axis=1)
    y2 = jnp.concatenate([buf[1, :, j, :] for j in range(N_SLAB)], axis=1)
    xn = x_ref[...] + mod_ref[0, 5:6, :] * (w1 * y1 + w2 * y2)
    o_ref[...] = _final_norm(xn, fg_ref[...]) if final else xn


def _combine(dest0, dest1, x, mods, slab, ys, rows_per_mod, final_g=None):
    m = x.shape[0]
    tc = COMBINE_TILE
    final = final_g is not None
    mod_map = ((lambda i, a, b: (i // (rows_per_mod // tc), 0, 0)) if rows_per_mod
               else (lambda i, a, b: (0, 0, 0)))
    in_specs = [pl.BlockSpec((tc, D_MODEL), lambda i, a, b: (i, 0)),
                pl.BlockSpec((1, 6, D_MODEL), mod_map),
                pl.BlockSpec((tc, LANE), lambda i, a, b: (i, 0))]
    args = [x, mods, slab]
    if final:
        in_specs.append(pl.BlockSpec((1, D_MODEL), lambda i, a, b: (0, 0)))
        args.append(final_g)
    in_specs.append(pl.BlockSpec(memory_space=pl.ANY))
    args.append(ys)
    return pl.pallas_call(
        functools.partial(_combine_kernel, final=final),
        out_shape=jax.ShapeDtypeStruct((m, D_MODEL), F32),
        grid_spec=pltpu.PrefetchScalarGridSpec(
            num_scalar_prefetch=2,
            grid=(m // tc,),
            in_specs=in_specs,
            out_specs=pl.BlockSpec((tc, D_MODEL), lambda i, a, b: (i, 0)),
            scratch_shapes=[pltpu.VMEM((2, tc, N_SLAB, LANE), F32), pltpu.SemaphoreType.DMA],
        ),
        compiler_params=_cparams(("arbitrary",)),
        name="combine",
    )(dest0, dest1, *args)


def _moe(h_slab, logits, x, mods, wg, wu, wd, rows_per_mod, final_g=None):
    m = x.shape[0]
    t = EXPERT_TILE
    n_rows = TOP_K * m + N_EXP * t
    slab, counts = _route(logits)
    cnt = counts[0, :N_EXP].astype(jnp.int32)
    padded = (cnt + (t - 1)) // t * t
    ends = jnp.cumsum(padded)
    starts = ends - padded
    dest = jnp.take(starts, slab[:, 0:2].astype(jnp.int32)) + slab[:, 2:4].astype(jnp.int32)
    dest0, dest1 = dest[:, 0], dest[:, 1]
    tile_start = jnp.arange(n_rows // t, dtype=jnp.int32) * t
    tile_expert = jnp.minimum(jnp.sum(tile_start[:, None] >= ends[None, :], axis=1), N_EXP - 1).astype(jnp.int32)
    zero_starts = jnp.concatenate([
        jnp.maximum(ends - t, 0),
        jnp.minimum(ends[-1] + jnp.arange(N_EXP, dtype=jnp.int32) * t, n_rows - t)]).astype(jnp.int32)
    xs = _dispatch(dest0, dest1, zero_starts, h_slab, n_rows)
    ys = _experts(tile_expert, xs, wg, wu, wd)
    return _combine(dest0, dest1, x, mods, slab, ys, rows_per_mod, final_g)


def _prep_w_in(w):
    qa, ka, va, ub, qc = (w[:, 0:512], w[:, 512:1024], w[:, 1024:1536], w[:, 1536:2048], w[:, 2048:2560])
    kc, vc = w[:, 2560:2688], w[:, 2688:2816]
    dup = lambda a: jnp.concatenate([a[:, 0:64], a[:, 0:64], a[:, 64:128], a[:, 64:128]], axis=1)
    scale = LOG2E * DHA ** -0.5
    return jnp.concatenate([qa * scale, ka, va, qc * scale, dup(kc), dup(vc), ub], axis=1).astype(BF16)


def _rope_tables(t):
    rows = t // GRID_W
    row = jnp.repeat(jnp.arange(rows, dtype=F32), GRID_W)
    col = jnp.tile(jnp.arange(GRID_W, dtype=F32), rows)
    inv = ROPE_BASE ** (-jnp.arange(0, ROT_HALF, 2, dtype=F32) / ROT_HALF)
    ang_r, ang_c = row[:, None] * inv, col[:, None] * inv
    zero = jnp.zeros_like(ang_r)
    seg = lambda a, b: jnp.concatenate([a, b], axis=1)
    cos = jnp.concatenate([seg(jnp.cos(ang_r), jnp.cos(ang_r)), seg(jnp.cos(ang_c), jnp.cos(ang_c))], axis=1)
    s_next = jnp.concatenate([seg(-jnp.sin(ang_r), zero), seg(-jnp.sin(ang_c), zero)], axis=1)
    s_prev = jnp.concatenate([seg(zero, jnp.sin(ang_r)), seg(zero, jnp.sin(ang_c))], axis=1)
    two = lambda a: jnp.concatenate([a, a], axis=1)
    return two(cos), two(s_next), two(s_prev)


def kernel(x_prompt, x_sample, cache_diff_k, cache_diff_v, cache_win_k, cache_win_v, c, c_ctx, w_ada, b_ada, norm1_g, norm2_g, w_in, w_gate, lam_p, subln_g, w_pool, pool_scale, sink, w_branch, w_out, w_ff_gate, w_ff_up, w_ff_down, w_router, w_ex_gate, w_ex_up, w_ex_down, final_g):
    bp, s, _ = x_prompt.shape
    bs, t, _ = x_sample.shape
    past = cache_diff_k.shape[2]

    cvec = jnp.concatenate([c_ctx[None, :], c, jnp.zeros((MOD_ROWS - 1 - bs, D_MODEL), F32)], axis=0)
    mods = _adaln(cvec, w_ada, b_ada).reshape(DEPTH, MOD_ROWS, 6, D_MODEL)
    rope_tabs = _rope_tables(t)

    ck_a = cache_diff_k.reshape(bs, DEPTH, past, HA * 2 * DHA)
    cv_a = cache_diff_v.reshape(bs, DEPTH, past, HA * 2 * DHA)
    ck_c = cache_win_k.reshape(bs, DEPTH, past, KVC * DHC)
    cv_c = cache_win_v.reshape(bs, DEPTH, past, KVC * DHC)

    xc = x_prompt.reshape(bp * s, D_MODEL)
    xl = x_sample.reshape(bs * t, D_MODEL)
    caches = [[], [], [], []]
    for l in range(DEPTH):
        last = l == DEPTH - 1
        lam_init = 0.8 - 0.6 * math.exp(-0.3 * l)
        w_in_l = _prep_w_in(w_in[l])
        wgate_l = w_gate[l].astype(BF16)
        wpool_l = w_pool[l].astype(BF16)
        wbr_l = w_branch[l].astype(BF16)
        wout_l = w_out[l].astype(BF16)
        g1 = norm1_g[l][None, :]
        g2 = norm2_g[l][None, :]
        sg = subln_g[l][None, :]
        ps = pool_scale[l][None, :]
        fg = final_g[None, :] if last else None
        i = l // 2
        if l % 2 == 0:
            wr = None
            wg, wu, wd = (w_ff_gate[i].astype(BF16), w_ff_up[i].astype(BF16), w_ff_down[i].astype(BF16))
        else:
            wr_f = jnp.pad(w_router[i], ((0, 0), (0, LANE - N_EXP)))
            wr_hi = wr_f.astype(BF16)
            wr = (wr_hi, (wr_f - wr_hi.astype(F32)).astype(BF16))
            wg, wu, wd = (w_ex_gate[i].astype(BF16), w_ex_up[i].astype(BF16), w_ex_down[i].astype(BF16))

        def channel_mixer(res, mod, rows_per_mod):
            if wr is None:
                assert fg is None
                return _ffn(res[1], res[0], mod, wg, wu, wd, tm=512, rows_per_mod=rows_per_mod)
            return _moe(res[1], res[2], res[0], mod, wg, wu, wd, rows_per_mod, final_g=fg)

        m_c = mods[l, 0:1]
        pc, ubc, dk, dv, wk, wv = _inproj(xc, g1, m_c, w_in_l, tm=256, rows_per_mod=0)
        for lst, a in zip(caches, (dk, dv, wk, wv)):
            lst.append(a)
        oa = _diff_ctx(pc, lam_p[l], sg, lam_init, bp, s)
        pooled = _pool(ubc, bp, s)
        oc = _gqa_ctx(pc, sink[l], bp, s)
        res = _merge(xc, oa, pooled, oc, g1, g2, m_c, wgate_l, wpool_l, ps, wbr_l, wout_l,
                     tm=256, rows_per_mod=0, wr=wr)
        xc = channel_mixer(res, m_c, 0)

        m_l = mods[l, 1:1 + bs]
        pl_, ubl = _inproj(xl, g1, m_l, w_in_l, tm=256, rows_per_mod=t, rope_tabs=rope_tabs)
        oa = _diff_lat(pl_, ck_a, cv_a, l, lam_p[l], sg, lam_init, bs, t, tq=128)
        pooled = _pool(ubl, bs, t)
        oc = _gqa_lat(pl_, ck_c, cv_c, l, sink[l], bs, t)
        res = _merge(xl, oa, pooled, oc, g1, g2, m_l, wgate_l, wpool_l, ps, wbr_l, wout_l,
                     tm=256, rows_per_mod=t, wr=wr)
        xl = channel_mixer(res, m_l, t)

    y_prompt = xc.reshape(bp, s, D_MODEL)
    y_sample = xl.reshape(bs, t, D_MODEL)
    new_diff_k = jnp.stack(caches[0], axis=0).reshape(DEPTH, bp, s, HA, 2, DHA).swapaxes(0, 1)
    new_diff_v = jnp.stack(caches[1], axis=0).reshape(DEPTH, bp, s, HA, 2 * DHA).swapaxes(0, 1)
    new_win_k = jnp.stack(caches[2], axis=0).reshape(DEPTH, bp, s, KVC, DHC).swapaxes(0, 1)
    new_win_v = jnp.stack(caches[3], axis=0).reshape(DEPTH, bp, s, KVC, DHC).swapaxes(0, 1)
    return (y_prompt, y_sample, new_diff_k, new_diff_v, new_win_k, new_win_v)
```
